```python
import math
import jax, jax.numpy as jnp
from jax import lax
import numpy as np

D_MODEL = 4096
BATCH = 4
SEQ = 4096
DEPTH = 1
DEC_BATCH = 8
DEC_SEQ = 2048
PAST_LEN = 128

DA_HEADS = 8
DA_HEAD_DIM = 128
DA_QK = DA_HEADS * 2 * DA_HEAD_DIM
DA_V = DA_HEADS * 2 * DA_HEAD_DIM
GLA_HEADS = 4
GLA_DK = 256
GLA_DV = 512
GLA_K = GLA_HEADS * GLA_DK
GLA_V = GLA_HEADS * GLA_DV
GLA_GATE_RANK = 16
GLA_GATE_TAU = 16.0
GLA_CHUNK = 64
D_FF = 11008
CONV_WIDTH = 3
REL_BUCKETS = 32
REL_MAX_DIST = 128
Q_BLOCK = 128
EPS = 1e-6
D_IN = DA_QK + DA_QK + DA_V + GLA_K + GLA_K + GLA_V + GLA_V + 2 * GLA_GATE_RANK + 2 * D_MODEL

kernel_name = 'hybrid_diffattn_gla_convffn_encoder'


def _rmsnorm(x, g):
    xf = x.astype(jnp.float32)
    y = xf * lax.rsqrt(jnp.mean(xf * xf, axis=-1, keepdims=True) + EPS)
    return (y * g.astype(jnp.float32)).astype(x.dtype)


def _t5_bucket(rel):
    half = REL_BUCKETS // 2
    max_exact = half // 2
    ret = jnp.where(rel > 0, half, 0)
    n = jnp.abs(rel)
    nf = jnp.maximum(n, 1).astype(jnp.float32)
    large = max_exact + (jnp.log(nf / max_exact) / math.log(REL_MAX_DIST / max_exact)
                         * (half - max_exact)).astype(jnp.int32)
    large = jnp.minimum(large, half - 1)
    return ret + jnp.where(n < max_exact, n, large)


def _diff_attention(q, k, v, rel_bias, lam):
    B, H, _, T, d = q.shape
    nq = T // Q_BLOCK
    qb = q.reshape(B, H, 2, nq, Q_BLOCK, d).transpose(3, 0, 1, 2, 4, 5)
    kpos = jnp.arange(T, dtype=jnp.int32)

    def block(args):
        i, qi = args
        qpos = i * Q_BLOCK + jnp.arange(Q_BLOCK, dtype=jnp.int32)
        bias = rel_bias[_t5_bucket(kpos[None, :] - qpos[:, None])]
        bias = jnp.transpose(bias, (2, 0, 1)).astype(jnp.float32)
        s = jnp.einsum('bhmqd,bhmkd->bhmqk', qi, k).astype(jnp.float32) + bias[None, :, None]
        p = jax.nn.softmax(s, axis=-1)
        w = p[:, :, 0] - lam * p[:, :, 1]
        return jnp.einsum('bhqk,bhkv->bhqv', w.astype(v.dtype), v)

    o = lax.map(block, (jnp.arange(nq, dtype=jnp.int32), qb))
    return o.transpose(1, 0, 3, 2, 4).reshape(B, T, H, 2 * d)


def _gla_chunked(q, k, v, log_a):
    B, H, T, dk = q.shape
    dv = v.shape[-1]
    C = GLA_CHUNK
    N = T // C
    q, k, log_a = (t.reshape(B, H, N, C, dk) for t in (q, k, log_a))
    v = v.reshape(B, H, N, C, dv)
    b = jnp.cumsum(log_a, axis=3)
    b_last = b[:, :, :, -1:, :]
    q_dec = q * jnp.exp(b)
    k_inv = k * jnp.exp(-b)
    mask = jnp.tril(jnp.ones((C, C), dtype=bool))
    att = jnp.where(mask, jnp.einsum('bhncd,bhnsd->bhncs', q_dec, k_inv), 0.0)
    o_intra = jnp.einsum('bhncs,bhnsv->bhncv', att, v)
    k_dec = k * jnp.exp(b_last - b)
    decay = jnp.exp(b_last[:, :, :, 0, :])

    def step(S, xs):
        qd, kd, vn, dn = xs
        o = jnp.einsum('bhcd,bhdv->bhcv', qd, S)
        S = dn[..., None] * S + jnp.einsum('bhcd,bhcv->bhdv', kd, vn)
        return S, o

    xs = tuple(jnp.moveaxis(t, 2, 0) for t in (q_dec, k_dec, v, decay))
    S0 = jnp.zeros((B, H, dk, dv), jnp.float32)
    _, o_inter = lax.scan(step, S0, xs)
    o = o_intra + jnp.moveaxis(o_inter, 0, 2)
    return o.reshape(B, H, T, dv)


def _depthwise_conv3(a, w, bias):
    ap = jnp.pad(a, ((0, 0), (1, 1), (0, 0)))
    return ap[:, :-2] * w[0] + ap[:, 1:-1] * w[1] + ap[:, 2:] * w[2] + bias


def _encoder_layer(x, lambda_init, rel_bias, g_mix, w_in, q_norm_g, k_norm_g, lambda_q1, lambda_k1,
                   lambda_q2, lambda_k2, da_subln_g, w_gate_fwd, b_gate_fwd, w_gate_bwd, b_gate_bwd,
                   gla_norm_g, w_branch_a, w_branch_b, w_out, g_ffn, w_up, conv_w, conv_b, w_down):
    B, T, _ = x.shape
    f32 = jnp.float32
    h = _rmsnorm(x, g_mix)
    z = h @ w_in
    widths = [DA_QK, DA_QK, DA_V, GLA_K, GLA_K, GLA_V, GLA_V, GLA_GATE_RANK, GLA_GATE_RANK, D_MODEL, D_MODEL]
    points = [int(c) for c in np.cumsum(widths)[:-1]]
    (da_q, da_k, da_v, gl_q, gl_k, gl_v, gl_r, gl_lf, gl_lb, gate_a, gate_b) = jnp.split(z, points, axis=-1)

    q = _rmsnorm(da_q.reshape(B, T, DA_HEADS, 2, DA_HEAD_DIM), q_norm_g) * (DA_HEAD_DIM ** -0.5)
    k = _rmsnorm(da_k.reshape(B, T, DA_HEADS, 2, DA_HEAD_DIM), k_norm_g)
    q = q.transpose(0, 2, 3, 1, 4)
    k = k.transpose(0, 2, 3, 1, 4)
    v = da_v.reshape(B, T, DA_HEADS, 2 * DA_HEAD_DIM).transpose(0, 2, 1, 3)
    lam = (jnp.exp(jnp.sum(lambda_q1.astype(f32) * lambda_k1.astype(f32)))
           - jnp.exp(jnp.sum(lambda_q2.astype(f32) * lambda_k2.astype(f32))) + lambda_init)
    o_a = _diff_attention(q, k, v, rel_bias, lam)
    o_a = (_rmsnorm(o_a, da_subln_g) * (1.0 - lambda_init)).reshape(B, T, DA_V)

    def heads(t, dh):
        return t.reshape(B, T, GLA_HEADS, dh).transpose(0, 2, 1, 3).astype(f32)

    gq = heads(gl_q, GLA_DK) * (GLA_DK ** -0.5)
    gk = heads(gl_k, GLA_DK)
    gv = heads(gl_v, GLA_DV)
    la_f = heads(jax.nn.log_sigmoid((gl_lf @ w_gate_fwd + b_gate_fwd).astype(f32)) / GLA_GATE_TAU, GLA_DK)
    la_b = heads(jax.nn.log_sigmoid((gl_lb @ w_gate_bwd + b_gate_bwd).astype(f32)) / GLA_GATE_TAU, GLA_DK)
    flip = lambda t: jnp.flip(t, axis=2)
    o_b = _gla_chunked(gq, gk, gv, la_f) + flip(_gla_chunked(flip(gq), flip(gk), flip(gv), flip(la_b)))
    o_b = _rmsnorm(o_b.transpose(0, 2, 1, 3), gla_norm_g).astype(x.dtype)
    o_b = (o_b * jax.nn.silu(gl_r.reshape(B, T, GLA_HEADS, GLA_DV))).reshape(B, T, GLA_V)

    merged = jax.nn.sigmoid(gate_a) * (o_a @ w_branch_a) + jax.nn.sigmoid(gate_b) * (o_b @ w_branch_b)
    x = x + merged @ w_out

    h2 = _rmsnorm(x, g_ffn)
    a, g = jnp.split(h2 @ w_up, 2, axis=-1)
    a = _depthwise_conv3(a, conv_w, conv_b)
    return x + (jax.nn.gelu(a) * g) @ w_down


def setup_inputs(seed: int = 0) -> dict:
    key = jax.random.key(seed)
    ks = jax.random.split(key, 25)
    f32 = jnp.float32
    L = DEPTH

    def nrm(k, shape, scale):
        return jax.random.normal(k, shape, f32) * scale

    def gain(k, shape):
        return 1.0 + 0.02 * jax.random.normal(k, shape, f32)

    return {
        'x_prompt': nrm(ks[0], (BATCH, SEQ, D_MODEL), 1.0),
        'x_sample': nrm(ks[1], (DEC_BATCH, DEC_SEQ, D_MODEL), 1.0),
        'rel_bias': nrm(ks[2], (REL_BUCKETS, DA_HEADS), 0.5),
        'g_mix': gain(ks[3], (L, D_MODEL)),
        'w_in': nrm(ks[4], (L, D_MODEL, D_IN), D_MODEL ** -0.5),
        'q_norm_g': gain(ks[5], (L, DA_HEAD_DIM)),
        'k_norm_g': gain(ks[6], (L, DA_HEAD_DIM)),
        'lambda_q1': nrm(ks[7], (L, DA_HEAD_DIM), 0.1),
        'lambda_k1': nrm(ks[8], (L, DA_HEAD_DIM), 0.1),
        'lambda_q2': nrm(ks[9], (L, DA_HEAD_DIM), 0.1),
        'lambda_k2': nrm(ks[10], (L, DA_HEAD_DIM), 0.1),
        'da_subln_g': gain(ks[11], (L, 2 * DA_HEAD_DIM)),
        'w_gate_fwd': nrm(ks[12], (L, GLA_GATE_RANK, GLA_K), GLA_GATE_RANK ** -0.5),
        'b_gate_fwd': nrm(ks[13], (L, GLA_K), 0.1),
        'w_gate_bwd': nrm(ks[14], (L, GLA_GATE_RANK, GLA_K), GLA_GATE_RANK ** -0.5),
        'b_gate_bwd': nrm(ks[15], (L, GLA_K), 0.1),
        'gla_norm_g': gain(ks[16], (L, GLA_DV)),
        'w_branch_a': nrm(ks[17], (L, DA_V, D_MODEL), DA_V ** -0.5),
        'w_branch_b': nrm(ks[18], (L, GLA_V, D_MODEL), GLA_V ** -0.5),
        'w_out': nrm(ks[19], (L, D_MODEL, D_MODEL), D_MODEL ** -0.5),
        'g_ffn': gain(ks[20], (L, D_MODEL)),
        'w_up': nrm(ks[21], (L, D_MODEL, 2 * D_FF), D_MODEL ** -0.5),
        'conv_w': nrm(ks[22], (L, CONV_WIDTH, D_FF), CONV_WIDTH ** -0.5),
        'conv_b': nrm(ks[23], (L, D_FF), 0.02),
        'w_down': nrm(ks[24], (L, D_FF, D_MODEL), D_FF ** -0.5),
    }


def reference(x_prompt, x_sample, rel_bias, g_mix, w_in, q_norm_g, k_norm_g, lambda_q1, lambda_k1,
              lambda_q2, lambda_k2, da_subln_g, w_gate_fwd, b_gate_fwd, w_gate_bwd, b_gate_bwd,
              gla_norm_g, w_branch_a, w_branch_b, w_out, g_ffn, w_up, conv_w, conv_b, w_down):
    y_prompt = x_prompt
    y_sample = x_sample
    for l in range(DEPTH):
        lambda_init = 0.8 - 0.6 * math.exp(-0.3 * l)
        lp = (g_mix[l], w_in[l], q_norm_g[l], k_norm_g[l], lambda_q1[l], lambda_k1[l], lambda_q2[l],
              lambda_k2[l], da_subln_g[l], w_gate_fwd[l], b_gate_fwd[l], w_gate_bwd[l], b_gate_bwd[l],
              gla_norm_g[l], w_branch_a[l], w_branch_b[l], w_out[l], g_ffn[l], w_up[l], conv_w[l],
              conv_b[l], w_down[l])
        y_prompt = _encoder_layer(y_prompt, lambda_init, rel_bias, *lp)
        y_sample = _encoder_layer(y_sample, lambda_init, rel_bias, *lp)
    return (y_prompt, y_sample)
```

```python
import functools
import math

import jax
import jax.numpy as jnp
from jax import lax
from jax.experimental import pallas as pl
from jax.experimental.pallas import tpu as pltpu

_EPS = 1e-6
_GLA_CHUNK = 64
_GLA_TAU = 16.0
_REL_MAX_DIST = 128
_LANES = 128
_HALO = 16
_V7X_VMEM_BYTES = 64 * 1024 * 1024
_VMEM_BUDGET = _V7X_VMEM_BYTES - 8 * 1024 * 1024

_F32 = jnp.float32
_BF16 = jnp.bfloat16
_NT = (((1,), (1,)), ((), ()))
_TN = (((0,), (0,)), ((), ()))


def _params(semantics, vmem_bytes):
    return pltpu.CompilerParams(dimension_semantics=semantics,
                                vmem_limit_bytes=int(min(vmem_bytes, _VMEM_BUDGET)))


def _tile(n, pref, align):
    t = min(pref, n)
    t -= t % align
    while t > align and n % t:
        t -= align
    assert t >= align and n % t == 0, (n, pref, align)
    return t


def _nbytes(shape, dtype):
    return math.prod(shape) * jnp.dtype(dtype).itemsize


def _dot(a, b):
    return jnp.dot(a, b, preferred_element_type=_F32)


def _rms(x, g):
    ms = jnp.mean(x * x, axis=-1, keepdims=True)
    return x * lax.rsqrt(ms + _EPS) * g


def _rmsnorm_body(x_ref, g_ref, o_ref):
    o_ref[...] = _rms(x_ref[...], g_ref[...]).astype(o_ref.dtype)


def _rmsnorm(x, g):
    m, d = x.shape
    tm = _tile(m, 256, 8)
    return pl.pallas_call(
        _rmsnorm_body,
        out_shape=jax.ShapeDtypeStruct((m, d), _BF16),
        grid=(m // tm,),
        in_specs=[pl.BlockSpec((tm, d), lambda i: (i, 0)), pl.BlockSpec((1, d), lambda i: (0, 0))],
        out_specs=pl.BlockSpec((tm, d), lambda i: (i, 0)),
        compiler_params=_params(("parallel",), 6 * _nbytes((tm, d), _F32) + (8 << 20)),
        name="rmsnorm",
    )(x, g.reshape(1, d))


def _rmsnorm_halo_body(x_ref, prev_ref, next_ref, g_ref, o_ref, *, tm, tr, tiles_per_seq):
    r = pl.program_id(1)
    row0 = pl.multiple_of(r * tr, tr)
    o_ref[pl.ds(row0, tr), :] = _rms(x_ref[...], g_ref[...]).astype(o_ref.dtype)

    @pl.when(r == 0)
    def _():
        pos = pl.program_id(0) % tiles_per_seq
        nxt = jnp.where(pos == tiles_per_seq - 1, 0.0, _rms(next_ref[...], g_ref[...]))
        prv = jnp.where(pos == 0, 0.0, _rms(prev_ref[...], g_ref[...]))
        o_ref[tm:tm + _HALO, :] = nxt.astype(o_ref.dtype)
        o_ref[tm + _HALO:tm + 2 * _HALO, :] = prv.astype(o_ref.dtype)


def _rmsnorm_halo(x, g, seq, tm):
    m, d = x.shape
    tr = _tile(tm, 256, _HALO)
    nt, nr = m // tm, tm // tr
    hb = tm // _HALO
    last = m // _HALO - 1
    return pl.pallas_call(
        functools.partial(_rmsnorm_halo_body, tm=tm, tr=tr, tiles_per_seq=seq // tm),
        out_shape=jax.ShapeDtypeStruct((nt, tm + 2 * _HALO, d), _BF16),
        grid=(nt, nr),
        in_specs=[
            pl.BlockSpec((tr, d), lambda i, r: (i * nr + r, 0)),
            pl.BlockSpec((_HALO, d), lambda i, r: (jnp.maximum(i * hb - 1, 0), 0)),
            pl.BlockSpec((_HALO, d), lambda i, r: (jnp.minimum((i + 1) * hb, last), 0)),
            pl.BlockSpec((1, d), lambda i, r: (0, 0)),
        ],
        out_specs=pl.BlockSpec((None, tm + 2 * _HALO, d), lambda i, r: (i, 0, 0)),
        compiler_params=_params(("parallel", "arbitrary"),
                                6 * _nbytes((tr, d), _F32) + 2 * _nbytes((tm + 2 * _HALO, d), _BF16) + (8 << 20)),
        name="rmsnorm_halo",
    )(x, x, x, g.reshape(1, d))


def _proj_cast_body(a_ref, b_ref, o_ref):
    o_ref[...] = _dot(a_ref[...], b_ref[...]).astype(o_ref.dtype)


def _proj_sigmoid_body(a_ref, b_ref, o_ref):
    o_ref[...] = jax.nn.sigmoid(_dot(a_ref[...], b_ref[...])).astype(o_ref.dtype)


def _proj_groupnorm_body(a_ref, b_ref, g_ref, o_ref, *, group):
    acc = _dot(a_ref[...], b_ref[...])
    for c in range(acc.shape[1] // group):
        cols = slice(c * group, (c + 1) * group)
        o_ref[:, cols] = _rms(acc[:, cols], g_ref[:, cols]).astype(o_ref.dtype)


def _proj_residual_body(a_ref, b_ref, r_ref, o_ref):
    o_ref[...] = r_ref[...] + _dot(a_ref[...], b_ref[...])


def _project(body, a, b, out_dtype, extra=(), extra_specs=(), tn_pref=1024, name="project"):
    m, k = a.shape
    n = b.shape[1]
    tm = _tile(m, 1024, 16)
    tn = _tile(n, tn_pref, _LANES)
    vmem = (2 * _nbytes((tm, k), a.dtype) + 2 * _nbytes((k, tn), b.dtype)
            + 6 * _nbytes((tm, tn), _F32) + (4 << 20))
    return pl.pallas_call(
        body,
        out_shape=jax.ShapeDtypeStruct((m, n), out_dtype),
        grid=(m // tm, n // tn),
        in_specs=[pl.BlockSpec((tm, k), lambda i, j: (i, 0)),
                  pl.BlockSpec((k, tn), lambda i, j: (0, j)),
                  *[spec(tm, tn) for spec in extra_specs]],
        out_specs=pl.BlockSpec((tm, tn), lambda i, j: (i, j)),
        compiler_params=_params(("parallel", "arbitrary"), vmem),
        name=name,
    )(a, b, *extra)


def _row_vec_spec(tm, tn):
    return pl.BlockSpec((1, tn), lambda i, j: (0, j))


def _tile_spec(tm, tn):
    return pl.BlockSpec((tm, tn), lambda i, j: (i, j))


def _t5_bucket(rel, num_buckets):
    half = num_buckets // 2
    max_exact = half // 2
    ret = jnp.where(rel > 0, half, 0)
    n = jnp.abs(rel)
    nf = jnp.maximum(n, 1).astype(_F32)
    large = max_exact + (jnp.log(nf / max_exact) / math.log(_REL_MAX_DIST / max_exact)
                         * (half - max_exact)).astype(jnp.int32)
    large = jnp.minimum(large, half - 1)
    return ret + jnp.where(n < max_exact, n, large)


def _band_body(rb_ref, o_ref, *, tq, tk, m_lo, num_buckets):
    h = pl.program_id(0)
    m = pl.program_id(1) + m_lo
    row = lax.broadcasted_iota(jnp.int32, (tq, tk), 0)
    col = lax.broadcasted_iota(jnp.int32, (tq, tk), 1)
    bucket = _t5_bucket(m * tq + col - row, num_buckets)
    out = jnp.zeros((tq, tk), _F32)
    for b in range(num_buckets):
        out = jnp.where(bucket == b, rb_ref[h, b], out)
    o_ref[...] = out


def _band_range(tq, tk):
    m_lo = -((_REL_MAX_DIST - 1 + tk + tq - 1) // tq)
    m_hi = (_REL_MAX_DIST - 1 + tq + tq - 1) // tq
    return m_lo, m_hi


def _bias_band(rel_bias, tq, tk):
    nbk, h = rel_bias.shape
    m_lo, m_hi = _band_range(tq, tk)
    nb = m_hi - m_lo + 1
    return pl.pallas_call(
        functools.partial(_band_body, tq=tq, tk=tk, m_lo=m_lo, num_buckets=nbk),
        out_shape=jax.ShapeDtypeStruct((h, nb, tq, tk), _F32),
        grid=(h, nb),
        in_specs=[pl.BlockSpec(memory_space=pltpu.SMEM)],
        out_specs=pl.BlockSpec((None, None, tq, tk), lambda i, j: (i, j, 0, 0)),
        compiler_params=_params(("parallel", "parallel"), 24 * _nbytes((tq, tk), _F32) + (4 << 20)),
        name="bias_band",
    )(rel_bias.T)


def _attn_body(lam_ref, q_ref, k_ref, v_ref, band_ref, g_ref, o_ref, acc1_ref, acc2_ref,
               *, tq, tk, d, m_lo, m_hi, lambda_init):
    i = pl.program_id(2)
    nk = k_ref.shape[0] // tk
    ratio = tk // tq
    q1 = q_ref[:, 0:d]
    q2 = q_ref[:, d:2 * d]
    acc1_ref[...] = jnp.zeros_like(acc1_ref)
    acc2_ref[...] = jnp.zeros_like(acc2_ref)

    def online(qm, km, v, bias, m_old, l_old, acc_ref):
        s = lax.dot_general(qm, km, _NT, preferred_element_type=_F32) + bias
        m_new = jnp.maximum(m_old, jnp.max(s, axis=-1, keepdims=True))
        alpha = jnp.exp(m_old - m_new)
        p = jnp.exp(s - m_new)
        acc_ref[...] = alpha * acc_ref[...] + _dot(p.astype(_BF16), v)
        return m_new, alpha * l_old + jnp.sum(p, axis=-1, keepdims=True)

    def step(j, carry):
        m1, l1, m2, l2 = carry
        off = pl.multiple_of(j * tk, tk)
        k = k_ref[pl.ds(off, tk), :]
        v = v_ref[pl.ds(off, tk), :]
        bias = band_ref[jnp.clip(j * ratio - i, m_lo, m_hi) - m_lo]
        m1, l1 = online(q1, k[:, 0:d], v, bias, m1, l1, acc1_ref)
        m2, l2 = online(q2, k[:, d:2 * d], v, bias, m2, l2, acc2_ref)
        return m1, l1, m2, l2

    neg = jnp.full((tq, 1), -jnp.inf, _F32)
    zero = jnp.zeros((tq, 1), _F32)
    _, l1, _, l2 = lax.fori_loop(0, nk, step, (neg, zero, neg, zero))

    lv = lam_ref[...]
    lam = (jnp.exp(jnp.sum(lv[0:1] * lv[1:2], axis=-1, keepdims=True))
           - jnp.exp(jnp.sum(lv[2:3] * lv[3:4], axis=-1, keepdims=True)) + lambda_init)
    o = acc1_ref[...] / l1 - lam * (acc2_ref[...] / l2)
    o_ref[...] = (_rms(o, g_ref[...]) * (1.0 - lambda_init)).astype(o_ref.dtype)


def _diff_attention(qk, v_src, band, lam_vecs, subln_g, batch, seq, heads, d, lambda_init, tq, tk):
    m = qk.shape[0]
    nq = seq // tq
    m_lo, m_hi = _band_range(tq, tk)
    nb = m_hi - m_lo + 1
    w = 2 * d
    vmem = (4 * _nbytes((seq, w), _BF16) + 2 * _nbytes((nb, tq, tk), _F32) + 2 * _nbytes((tq, w), _F32)
            + 16 * _nbytes((tq, tk), _F32) + (6 << 20))
    return pl.pallas_call(
        functools.partial(_attn_body, tq=tq, tk=tk, d=d, m_lo=m_lo, m_hi=m_hi, lambda_init=lambda_init),
        out_shape=jax.ShapeDtypeStruct((m, heads * w), _BF16),
        grid=(batch, heads, nq),
        in_specs=[
            pl.BlockSpec((4, d), lambda b, h, i: (0, 0)),
            pl.BlockSpec((tq, w), lambda b, h, i: (b * nq + i, h)),
            pl.BlockSpec((seq, w), lambda b, h, i: (b, heads + h)),
            pl.BlockSpec((seq, w), lambda b, h, i: (b, h)),
            pl.BlockSpec((None, nb, tq, tk), lambda b, h, i: (h, 0, 0, 0)),
            pl.BlockSpec((1, w), lambda b, h, i: (0, 0)),
        ],
        out_specs=pl.BlockSpec((tq, w), lambda b, h, i: (b * nq + i, h)),
        scratch_shapes=[pltpu.VMEM((tq, w), _F32), pltpu.VMEM((tq, w), _F32)],
        compiler_params=_params(("parallel", "parallel", "arbitrary"), vmem),
        name="diff_attention",
    )(lam_vecs, qk, qk, v_src, band, subln_g.reshape(1, w))


def _split_dot(lhs, x, dims=None):
    hi = x.astype(_BF16)
    lo = (x - hi.astype(_F32)).astype(_BF16)
    if dims is None:
        return _dot(lhs, hi) + _dot(lhs, lo)
    return (lax.dot_general(hi, lhs, dims, preferred_element_type=_F32)
            + lax.dot_general(lo, lhs, dims, preferred_element_type=_F32))


def _gla_body(*refs, tt, dk, reverse, final, q_scale):
    if final:
        (q_ref, k_ref, lr_ref, v_ref, wg_ref, bg_ref, fwd_ref, r_ref, gn_ref, o_ref, s_ref) = refs
    else:
        (q_ref, k_ref, lr_ref, v_ref, wg_ref, bg_ref, o_ref, s_ref) = refs
    c_len = _GLA_CHUNK
    nc = tt // c_len

    @pl.when(pl.program_id(2) == 0)
    def _():
        s_ref[...] = jnp.zeros_like(s_ref)

    q = q_ref[...] * q_scale
    k = k_ref[...]
    v = v_ref[...]
    pre = _dot(lr_ref[...].astype(_BF16), wg_ref[...]) + bg_ref[...]
    log_a = (jnp.minimum(pre, 0.0) - jnp.log1p(jnp.exp(-jnp.abs(pre)))) / _GLA_TAU

    row = lax.broadcasted_iota(jnp.int32, (tt, tt), 0)
    col = lax.broadcasted_iota(jnp.int32, (tt, tt), 1)
    same = (row // c_len) == (col // c_len)
    causal = same & ((col >= row) if reverse else (col <= row))
    b = _split_dot(causal.astype(_BF16), log_a)
    b_tot = _split_dot(same.astype(_BF16), log_a)
    sel = (lax.broadcasted_iota(jnp.int32, (tt, _LANES), 0) // c_len
           == lax.broadcasted_iota(jnp.int32, (tt, _LANES), 1)).astype(_BF16)
    decay_cols = jnp.exp(_split_dot(sel, log_a, _TN))

    q_dec = (q * jnp.exp(b)).astype(_BF16)
    k_inv = (k * jnp.exp(-b)).astype(_BF16)
    k_dec = (k * jnp.exp(b_tot - b)).astype(_BF16)
    att = lax.dot_general(q_dec, k_inv, _NT, preferred_element_type=_F32)
    o_intra = _dot(jnp.where(causal, att, 0.0).astype(_BF16), v)

    outs = [None] * nc
    for c in (reversed(range(nc)) if reverse else range(nc)):
        rows = slice(c * c_len, (c + 1) * c_len)
        s = s_ref[...]
        outs[c] = o_intra[rows] + _dot(q_dec[rows], s.astype(_BF16))
        s_ref[...] = decay_cols[:, c:c + 1] * s + lax.dot_general(k_dec[rows], v[rows], _TN,
                                                                 preferred_element_type=_F32)
    o = jnp.concatenate(outs, axis=0)

    if final:
        o = _rms(o + fwd_ref[...], gn_ref[...])
        r = r_ref[...].astype(_F32)
        o_ref[...] = (o * (r * jax.nn.sigmoid(r))).astype(o_ref.dtype)
    else:
        o_ref[...] = o


def _gla(qkl, vgr, w_gate, b_gate, batch, seq, heads, dk, dv, v_col0, reverse,
         fwd=None, r_col0=None, norm_g=None):
    m = qkl.shape[0]
    tt = _tile(seq, 256, _GLA_CHUNK)
    nt = seq // tt
    final = fwd is not None

    def rows(b, h, t):
        return b * nt + (nt - 1 - t if reverse else t)

    in_specs = [
        pl.BlockSpec((tt, dk), lambda b, h, t: (rows(b, h, t), h)),
        pl.BlockSpec((tt, dk), lambda b, h, t: (rows(b, h, t), heads + h)),
        pl.BlockSpec((tt, dk), lambda b, h, t: (rows(b, h, t), 2 * heads)),
        pl.BlockSpec((tt, dv), lambda b, h, t: (rows(b, h, t), v_col0 + h)),
        pl.BlockSpec((dk, dk), lambda b, h, t: (0, h)),
        pl.BlockSpec((1, dk), lambda b, h, t: (0, h)),
    ]
    args = [qkl, qkl, qkl, vgr, w_gate, b_gate]
    if final:
        in_specs += [
            pl.BlockSpec((tt, dv), lambda b, h, t: (rows(b, h, t), h)),
            pl.BlockSpec((tt, dv), lambda b, h, t: (rows(b, h, t), r_col0 + h)),
            pl.BlockSpec((1, dv), lambda b, h, t: (0, 0)),
        ]
        args += [fwd, vgr, norm_g.reshape(1, dv)]
    return pl.pallas_call(
        functools.partial(_gla_body, tt=tt, dk=dk, reverse=reverse, final=final, q_scale=dk ** -0.5),
        out_shape=jax.ShapeDtypeStruct((m, heads * dv), _BF16 if final else _F32),
        grid=(batch, heads, nt),
        in_specs=in_specs,
        out_specs=pl.BlockSpec((tt, dv), lambda b, h, t: (rows(b, h, t), h)),
        scratch_shapes=[pltpu.VMEM((dk, dv), _F32)],
        compiler_params=_params(("parallel", "parallel", "arbitrary"), 32 << 20),
        name="gla_bwd" if reverse else "gla_fwd",
    )(*args)


def _merge_body(oa_ref, ob_ref, pa_ref, pb_ref, ga_ref, gb_ref, o_ref):
    a = _dot(oa_ref[...], pa_ref[...])
    b = _dot(ob_ref[...], pb_ref[...])
    o_ref[...] = (ga_ref[...].astype(_F32) * a + gb_ref[...].astype(_F32) * b).astype(o_ref.dtype)


def _merge(o_a, o_b, p_a, p_b, gates):
    m, ka = o_a.shape
    kb = o_b.shape[1]
    n = p_a.shape[1]
    tm = _tile(m, 1024, 16)
    tn = _tile(n, 1024, _LANES)
    nj = n // tn
    vmem = (2 * _nbytes((tm, ka + kb), _BF16) + 2 * _nbytes((ka + kb, tn), _BF16)
            + 4 * _nbytes((tm, tn), _BF16) + 6 * _nbytes((tm, tn), _F32) + (4 << 20))
    return pl.pallas_call(
        _merge_body,
        out_shape=jax.ShapeDtypeStruct((m, n), _BF16),
        grid=(m // tm, nj),
        in_specs=[
            pl.BlockSpec((tm, ka), lambda i, j: (i, 0)),
            pl.BlockSpec((tm, kb), lambda i, j: (i, 0)),
            pl.BlockSpec((ka, tn), lambda i, j: (0, j)),
            pl.BlockSpec((kb, tn), lambda i, j: (0, j)),
            pl.BlockSpec((tm, tn), lambda i, j: (i, j)),
            pl.BlockSpec((tm, tn), lambda i, j: (i, nj + j)),
        ],
        out_specs=pl.BlockSpec((tm, tn), lambda i, j: (i, j)),
        compiler_params=_params(("parallel", "arbitrary"), vmem),
        name="branch_merge",
    )(o_a, o_b, p_a, p_b, gates, gates)


def _gelu_tanh(x):
    return x * (0.5 * (1.0 + jnp.tanh(math.sqrt(2.0 / math.pi) * (x + 0.044715 * (x * x * x)))))


def _ffn_up_body(h_ref, wa_ref, wb_ref, cw_ref, cb_ref, o_ref, *, tm):
    a = _dot(h_ref[...], wa_ref[...])
    g = _dot(h_ref[0:tm, :], wb_ref[...])
    rows = a.shape[0]
    prev = pltpu.roll(a, 1, 0)[0:tm]
    nxt = pltpu.roll(a, rows - 1, 0)[0:tm]
    conv = prev * cw_ref[0:1, :] + a[0:tm] * cw_ref[1:2, :] + nxt * cw_ref[2:3, :] + cb_ref[...]
    o_ref[...] = (_gelu_tanh(conv) * g).astype(o_ref.dtype)


def _ffn_up(h_tiles, w_a, w_b, conv_w, conv_b, tm):
    nt, rows, d = h_tiles.shape
    f = w_a.shape[1]
    tf = _tile(f, 512, _LANES)
    vmem = (2 * _nbytes((rows, d), _BF16) + 4 * _nbytes((d, tf), _BF16) + 2 * _nbytes((tm, tf), _BF16)
            + 8 * _nbytes((rows, tf), _F32) + (4 << 20))
    return pl.pallas_call(
        functools.partial(_ffn_up_body, tm=tm),
        out_shape=jax.ShapeDtypeStruct((nt * tm, f), _BF16),
        grid=(nt, f // tf),
        in_specs=[
            pl.BlockSpec((None, rows, d), lambda i, j: (i, 0, 0)),
            pl.BlockSpec((d, tf), lambda i, j: (0, j)),
            pl.BlockSpec((d, tf), lambda i, j: (0, j)),
            pl.BlockSpec((3, tf), lambda i, j: (0, j)),
            pl.BlockSpec((1, tf), lambda i, j: (0, j)),
        ],
        out_specs=pl.BlockSpec((tm, tf), lambda i, j: (i, j)),
        compiler_params=_params(("parallel", "arbitrary"), vmem),
        name="ffn_up",
    )(h_tiles, w_a, w_b, conv_w, conv_b)


def _ffn_down_body(a_ref, w_ref, r_ref, o_ref):
    @pl.when(pl.program_id(2) == 0)
    def _():
        o_ref[...] = r_ref[...]

    o_ref[...] += _dot(a_ref[...], w_ref[...])


def _ffn_down(act, w, resid):
    m, f = act.shape
    n = w.shape[1]
    tm = _tile(m, 1024, 16)
    tn = _tile(n, 1024, _LANES)
    tk = _tile(f, 2816, _LANES)
    vmem = (2 * _nbytes((tm, tk), _BF16) + 2 * _nbytes((tk, tn), _BF16) + 6 * _nbytes((tm, tn), _F32) + (4 << 20))
    return pl.pallas_call(
        _ffn_down_body,
        out_shape=jax.ShapeDtypeStruct((m, n), _F32),
        grid=(m // tm, n // tn, f // tk),
        in_specs=[
            pl.BlockSpec((tm, tk), lambda i, j, k: (i, k)),
            pl.BlockSpec((tk, tn), lambda i, j, k: (k, j)),
            pl.BlockSpec((tm, tn), lambda i, j, k: (i, j)),
        ],
        out_specs=pl.BlockSpec((tm, tn), lambda i, j, k: (i, j)),
        compiler_params=_params(("parallel", "parallel", "arbitrary"), vmem),
        name="ffn_down",
    )(act, w, resid)


def _pad_cols(w, n):
    return jnp.pad(w, ((0, 0), (0, n - w.shape[1])))


def _prepare_weights(lambda_init, rel_bias, g_mix, w_in, q_norm_g, k_norm_g, lambda_q1, lambda_k1, lambda_q2,
                     lambda_k2, da_subln_g, w_gate_fwd, b_gate_fwd, w_gate_bwd, b_gate_bwd, gla_norm_g,
                     w_branch_a, w_branch_b, w_out, g_ffn, w_up, conv_w, conv_b, w_down):
    heads = rel_bias.shape[1]
    d = q_norm_g.shape[-1]
    da = heads * 2 * d
    rank, gla_k = w_gate_fwd.shape
    dv = gla_norm_g.shape[-1]
    gla_v = w_branch_b.shape[0]
    gh = gla_v // dv
    dk = gla_k // gh
    f = conv_b.shape[-1]
    d_model = w_in.shape[0]
    widths = [da, da, da, gla_k, gla_k, gla_v, gla_v, rank, rank, d_model, d_model]
    starts = [0]
    for wd in widths:
        starts.append(starts[-1] + wd)
    col = lambda s: w_in[:, starts[s]:starts[s + 1]].astype(_BF16)

    w_qk = jnp.concatenate([col(0), col(1)], axis=1)
    w_vgr = jnp.concatenate([col(2), col(5), col(6)], axis=1)
    w_qkl = jnp.concatenate([col(3), col(4), _pad_cols(jnp.concatenate([col(7), col(8)], axis=1), dk)], axis=1)
    w_gates = jnp.concatenate([col(9), col(10)], axis=1)
    qk_gain = jnp.concatenate([jnp.tile(q_norm_g * (d ** -0.5), 2 * heads), jnp.tile(k_norm_g, 2 * heads)])

    def gate_weight(w, first_row):
        return jnp.zeros((dk, gla_k), _BF16).at[first_row:first_row + rank].set(w.astype(_BF16))

    fp = -(-f // 1024) * 1024
    return dict(
        heads=heads, d=d, gh=gh, dk=dk, dv=dv, lambda_init=lambda_init,
        rel_bias=rel_bias, g_mix=g_mix, g_ffn=g_ffn,
        w_qk=w_qk, w_vgr=w_vgr, w_qkl=w_qkl, w_gates=w_gates, qk_gain=qk_gain.reshape(1, -1),
        lam_vecs=jnp.stack([lambda_q1, lambda_k1, lambda_q2, lambda_k2]),
        da_subln_g=da_subln_g, gla_norm_g=gla_norm_g,
        wg_fwd=gate_weight(w_gate_fwd, 0), wg_bwd=gate_weight(w_gate_bwd, rank),
        bg_fwd=b_gate_fwd.reshape(1, -1), bg_bwd=b_gate_bwd.reshape(1, -1),
        p_a=w_branch_a.astype(_BF16), p_b=w_branch_b.astype(_BF16), w_out=w_out.astype(_BF16),
        w_up_a=_pad_cols(w_up[:, :f].astype(_BF16), fp), w_up_b=_pad_cols(w_up[:, f:].astype(_BF16), fp),
        conv_w=_pad_cols(conv_w, fp), conv_b=_pad_cols(conv_b.reshape(1, f), fp),
        w_down=jnp.pad(w_down.astype(_BF16), ((0, fp - f), (0, 0))),
    )


def _encoder_layer(x3, p, band, tq, tk):
    batch, seq, d_model = x3.shape
    x = x3.reshape(batch * seq, d_model)
    heads, d, gh, dk, dv = p["heads"], p["d"], p["gh"], p["dk"], p["dv"]

    h = _rmsnorm(x, p["g_mix"])
    qk = _project(functools.partial(_proj_groupnorm_body, group=d), h, p["w_qk"], _BF16,
                  extra=(p["qk_gain"],), extra_specs=(_row_vec_spec,), name="proj_qk")
    vgr = _project(_proj_cast_body, h, p["w_vgr"], _BF16, name="proj_vgr")
    qkl = _project(_proj_cast_body, h, p["w_qkl"], _F32, tn_pref=768, name="proj_qkl")
    gates = _project(_proj_sigmoid_body, h, p["w_gates"], _BF16, name="proj_gates")

    o_a = _diff_attention(qk, vgr, band, p["lam_vecs"], p["da_subln_g"], batch, seq, heads, d,
                          p["lambda_init"], tq, tk)

    v_col0 = (heads * 2 * d) // dv
    r_col0 = v_col0 + gh
    fwd = _gla(qkl, vgr, p["wg_fwd"], p["bg_fwd"], batch, seq, gh, dk, dv, v_col0, reverse=False)
    o_b = _gla(qkl, vgr, p["wg_bwd"], p["bg_bwd"], batch, seq, gh, dk, dv, v_col0, reverse=True,
               fwd=fwd, r_col0=r_col0, norm_g=p["gla_norm_g"])

    merged = _merge(o_a, o_b, p["p_a"], p["p_b"], gates)
    x1 = _project(_proj_residual_body, merged, p["w_out"], _F32, extra=(x,), extra_specs=(_tile_spec,),
                  name="proj_out")

    tm = _tile(seq, 1024, 2 * _HALO)
    h2 = _rmsnorm_halo(x1, p["g_ffn"], seq, tm)
    act = _ffn_up(h2, p["w_up_a"], p["w_up_b"], p["conv_w"], p["conv_b"], tm)
    y = _ffn_down(act, p["w_down"], x1)
    return y.reshape(batch, seq, d_model)


def kernel(x_prompt, x_sample, rel_bias, g_mix, w_in, q_norm_g, k_norm_g, lambda_q1, lambda_k1, lambda_q2,
           lambda_k2, da_subln_g, w_gate_fwd, b_gate_fwd, w_gate_bwd, b_gate_bwd, gla_norm_g, w_branch_a,
           w_branch_b, w_out, g_ffn, w_up, conv_w, conv_b, w_down):
    layer_weights = (g_mix, w_in, q_norm_g, k_norm_g, lambda_q1, lambda_k1, lambda_q2, lambda_k2, da_subln_g,
                     w_gate_fwd, b_gate_fwd, w_gate_bwd, b_gate_bwd, gla_norm_g, w_branch_a, w_branch_b, w_out,
                     g_ffn, w_up, conv_w, conv_b, w_down)
    min_seq = min(x_prompt.shape[1], x_sample.shape[1])
    tq = _tile(min_seq, 256, _LANES)
    tk = _tile(min_seq, 512, tq)
    band = _bias_band(rel_bias, tq, tk)
    y_prompt, y_sample = x_prompt, x_sample
    for l in range(g_mix.shape[0]):
        lambda_init = 0.8 - 0.6 * math.exp(-0.3 * l)
        p = _prepare_weights(lambda_init, rel_bias, *(w[l] for w in layer_weights))
        y_prompt = _encoder_layer(y_prompt, p, band, tq, tk)
        y_sample = _encoder_layer(y_sample, p, band, tq, tk)
    return (y_prompt, y_sample)
```

```python
import functools
import math

import jax
import jax.numpy as jnp
from jax import lax
from jax.experimental import pallas as pl
from jax.experimental.pallas import tpu as pltpu

_EPS = 1e-6
_GLA_CHUNK = 64
_GLA_TAU = 16.0
_REL_MAX_DIST = 128
_LOG2E = math.log2(math.e)
_LANES = 128
_HALO = 16
_ROW_GROUP = 16
_V7X_VMEM_BYTES = 64 * 1024 * 1024
_VMEM_BUDGET = _V7X_VMEM_BYTES - 8 * 1024 * 1024

_F32 = jnp.float32
_BF16 = jnp.bfloat16
_NT = (((1,), (1,)), ((), ()))
_TN = (((0,), (0,)), ((), ()))


def _params(semantics, vmem_bytes):
    return pltpu.CompilerParams(dimension_semantics=semantics,
                                vmem_limit_bytes=int(min(vmem_bytes, _VMEM_BUDGET)))


def _tile(n, pref, align):
    t = min(pref, n)
    t -= t % align
    while t > align and n % t:
        t -= align
    assert t >= align and n % t == 0, (n, pref, align)
    return t


def _nbytes(shape, dtype):
    return math.prod(shape) * jnp.dtype(dtype).itemsize


def _dot(a, b):
    return jnp.dot(a, b, preferred_element_type=_F32)


def _rms(x, g):
    ms = jnp.mean(x * x, axis=-1, keepdims=True)
    return x * lax.rsqrt(ms + _EPS) * g


def _rmsnorm_body(x_ref, g_ref, o_ref):
    o_ref[...] = _rms(x_ref[...], g_ref[...]).astype(o_ref.dtype)


def _rmsnorm(x, g):
    m, d = x.shape
    tm = _tile(m, 256, 8)
    return pl.pallas_call(
        _rmsnorm_body,
        out_shape=jax.ShapeDtypeStruct((m, d), _BF16),
        grid=(m // tm,),
        in_specs=[pl.BlockSpec((tm, d), lambda i: (i, 0)), pl.BlockSpec((1, d), lambda i: (0, 0))],
        out_specs=pl.BlockSpec((tm, d), lambda i: (i, 0)),
        compiler_params=_params(("parallel",), 6 * _nbytes((tm, d), _F32) + (8 << 20)),
        name="rmsnorm",
    )(x, g.reshape(1, d))


def _rmsnorm_halo_body(x_ref, prev_ref, next_ref, g_ref, o_ref, *, tm, tr, tiles_per_seq):
    r = pl.program_id(1)
    row0 = pl.multiple_of(r * tr, tr)
    o_ref[pl.ds(row0, tr), :] = _rms(x_ref[...], g_ref[...]).astype(o_ref.dtype)

    @pl.when(r == 0)
    def _():
        pos = pl.program_id(0) % tiles_per_seq
        nxt = jnp.where(pos == tiles_per_seq - 1, 0.0, _rms(next_ref[...], g_ref[...]))
        prv = jnp.where(pos == 0, 0.0, _rms(prev_ref[...], g_ref[...]))
        o_ref[tm:tm + _HALO, :] = nxt.astype(o_ref.dtype)
        o_ref[tm + _HALO:tm + 2 * _HALO, :] = prv.astype(o_ref.dtype)


def _rmsnorm_halo(x, g, seq, tm):
    m, d = x.shape
    tr = _tile(tm, 256, _HALO)
    nt, nr = m // tm, tm // tr
    hb = tm // _HALO
    last = m // _HALO - 1
    return pl.pallas_call(
        functools.partial(_rmsnorm_halo_body, tm=tm, tr=tr, tiles_per_seq=seq // tm),
        out_shape=jax.ShapeDtypeStruct((nt, tm + 2 * _HALO, d), _BF16),
        grid=(nt, nr),
        in_specs=[
            pl.BlockSpec((tr, d), lambda i, r: (i * nr + r, 0)),
            pl.BlockSpec((_HALO, d), lambda i, r: (jnp.maximum(i * hb - 1, 0), 0)),
            pl.BlockSpec((_HALO, d), lambda i, r: (jnp.minimum((i + 1) * hb, last), 0)),
            pl.BlockSpec((1, d), lambda i, r: (0, 0)),
        ],
        out_specs=pl.BlockSpec((None, tm + 2 * _HALO, d), lambda i, r: (i, 0, 0)),
        compiler_params=_params(("parallel", "arbitrary"),
                                6 * _nbytes((tr, d), _F32) + 2 * _nbytes((tm + 2 * _HALO, d), _BF16) + (8 << 20)),
        name="rmsnorm_halo",
    )(x, x, x, g.reshape(1, d))


def _proj_cast_body(a_ref, b_ref, o_ref):
    o_ref[...] = _dot(a_ref[...], b_ref[...]).astype(o_ref.dtype)


def _proj_sigmoid_body(a_ref, b_ref, o_ref):
    o_ref[...] = jax.nn.sigmoid(_dot(a_ref[...], b_ref[...])).astype(o_ref.dtype)


def _proj_groupnorm_body(a_ref, b_ref, g_ref, o_ref, *, group):
    acc = _dot(a_ref[...], b_ref[...])
    for c in range(acc.shape[1] // group):
        cols = slice(c * group, (c + 1) * group)
        o_ref[:, cols] = _rms(acc[:, cols], g_ref[:, cols]).astype(o_ref.dtype)


def _proj_residual_body(a_ref, b_ref, r_ref, o_ref):
    o_ref[...] = r_ref[...] + _dot(a_ref[...], b_ref[...])


def _project(body, a, b, out_dtype, sections=None, extra=(), extra_specs=(), name="project"):
    m, k = a.shape
    sections = sections or ((0, b.shape[1]),)
    n = sum(width for _, width in sections)
    tm = _tile(m, 1024, 16)
    tn = _tile(math.gcd(*(v for sec in sections for v in sec)), 1024, _LANES)
    first_out, shifts, done = [], [], 0
    for start, width in sections:
        first_out.append(done // tn)
        shifts.append(start // tn - done // tn)
        done += width

    def src_block(j):
        shift = shifts[0]
        for first, s in zip(first_out[1:], shifts[1:]):
            shift = jnp.where(j >= first, s, shift)
        return j + shift

    vmem = (2 * _nbytes((tm, k), a.dtype) + 2 * _nbytes((k, tn), b.dtype)
            + 6 * _nbytes((tm, tn), _F32) + (4 << 20))
    return pl.pallas_call(
        body,
        out_shape=jax.ShapeDtypeStruct((m, n), out_dtype),
        grid=(m // tm, n // tn),
        in_specs=[pl.BlockSpec((tm, k), lambda i, j: (i, 0)),
                  pl.BlockSpec((k, tn), lambda i, j: (0, src_block(j))),
                  *[spec(tm, tn) for spec in extra_specs]],
        out_specs=pl.BlockSpec((tm, tn), lambda i, j: (i, j)),
        compiler_params=_params(("parallel", "arbitrary"), vmem),
        name=name,
    )(a, b, *extra)


def _row_vec_spec(tm, tn):
    return pl.BlockSpec((1, tn), lambda i, j: (0, j))


def _tile_spec(tm, tn):
    return pl.BlockSpec((tm, tn), lambda i, j: (i, j))


def _t5_bucket(rel, num_buckets):
    half = num_buckets // 2
    max_exact = half // 2
    ret = jnp.where(rel > 0, half, 0)
    n = jnp.abs(rel)
    nf = jnp.maximum(n, 1).astype(_F32)
    large = max_exact + (jnp.log(nf / max_exact) / math.log(_REL_MAX_DIST / max_exact)
                         * (half - max_exact)).astype(jnp.int32)
    large = jnp.minimum(large, half - 1)
    return ret + jnp.where(n < max_exact, n, large)


def _band_body(rb_ref, o_ref, *, tq, tk, m_lo, num_buckets):
    h = pl.program_id(0)
    m = pl.program_id(1) + m_lo
    row = lax.broadcasted_iota(jnp.int32, (tq, tk), 0)
    col = lax.broadcasted_iota(jnp.int32, (tq, tk), 1)
    bucket = _t5_bucket(m * tq + col - row, num_buckets)
    out = jnp.zeros((tq, tk), _F32)
    for b in range(num_buckets):
        out = jnp.where(bucket == b, rb_ref[h, b], out)
    o_ref[...] = out * _LOG2E


def _band_range(tq, tk):
    m_lo = -((_REL_MAX_DIST - 1 + tk + tq - 1) // tq)
    m_hi = (_REL_MAX_DIST - 1 + tq + tq - 1) // tq
    return m_lo, m_hi


def _bias_band(rel_bias, tq, tk):
    nbk, h = rel_bias.shape
    m_lo, m_hi = _band_range(tq, tk)
    nb = m_hi - m_lo + 1
    return pl.pallas_call(
        functools.partial(_band_body, tq=tq, tk=tk, m_lo=m_lo, num_buckets=nbk),
        out_shape=jax.ShapeDtypeStruct((h, nb, tq, tk), _F32),
        grid=(h, nb),
        in_specs=[pl.BlockSpec(memory_space=pltpu.SMEM)],
        out_specs=pl.BlockSpec((None, None, tq, tk), lambda i, j: (i, j, 0, 0)),
        compiler_params=_params(("parallel", "parallel"), 24 * _nbytes((tq, tk), _F32) + (4 << 20)),
        name="bias_band",
    )(rel_bias.T)


def _lane_fold(x, op):
    return functools.reduce(op, [x[:, c:c + _LANES] for c in range(0, x.shape[1], _LANES)])


def _attn_body(lam_ref, q_ref, k_ref, v_ref, band_ref, g_ref, o_ref, s_ref, p_ref,
               *, tq, tk, d, m_lo, m_hi, lambda_init):
    i = pl.program_id(2)
    nk = k_ref.shape[0] // tk
    ratio = tk // tq

    band_idx = [jnp.clip(j * ratio - i, m_lo, m_hi) - m_lo for j in range(nk)]

    def scores(mp):
        cols = slice(mp * d, (mp + 1) * d)
        qm = q_ref[:, cols]
        for j in range(nk):
            keys = slice(j * tk, (j + 1) * tk)
            s_ref[mp, :, keys] = lax.dot_general(qm, k_ref[keys, cols], _NT, preferred_element_type=_F32)

    def softmax_rows(mp):
        sums = []
        for r in range(0, tq, _ROW_GROUP):
            rows = slice(r, r + _ROW_GROUP)
            part_max = None
            for j in range(nk):
                keys = slice(j * tk, (j + 1) * tk)
                folded = _lane_fold(s_ref[mp, rows, keys] + band_ref[band_idx[j], rows, :], jnp.maximum)
                part_max = folded if part_max is None else jnp.maximum(part_max, folded)
            row_max = jnp.max(part_max, axis=-1, keepdims=True)
            part_sum = jnp.zeros((_ROW_GROUP, _LANES), _F32)
            for j in range(nk):
                keys = slice(j * tk, (j + 1) * tk)
                p = jnp.exp2(s_ref[mp, rows, keys] + (band_ref[band_idx[j], rows, :] - row_max))
                part_sum = part_sum + _lane_fold(p, jnp.add)
                p_ref[mp, rows, keys] = p.astype(p_ref.dtype)
            sums.append(jnp.sum(part_sum, axis=-1, keepdims=True))
        return jnp.concatenate(sums, axis=0)

    scores(0)
    scores(1)
    heads_out = []
    for mp in range(2):
        row_sum = softmax_rows(mp)
        heads_out.append(_dot(p_ref[mp], v_ref[...]) / row_sum)

    lv = lam_ref[...]
    lam = (jnp.exp(jnp.sum(lv[0:1] * lv[1:2], axis=-1, keepdims=True))
           - jnp.exp(jnp.sum(lv[2:3] * lv[3:4], axis=-1, keepdims=True)) + lambda_init)
    o = heads_out[0] - lam * heads_out[1]
    o_ref[...] = (_rms(o, g_ref[...]) * (1.0 - lambda_init)).astype(o_ref.dtype)


def _diff_attention(qk, v_src, band, lam_vecs, subln_g, batch, seq, heads, d, lambda_init, tq, tk):
    m = qk.shape[0]
    nq = seq // tq
    m_lo, m_hi = _band_range(tq, tk)
    nb = m_hi - m_lo + 1
    w = 2 * d
    vmem = (4 * _nbytes((seq, w), _BF16) + 2 * _nbytes((nb, tq, tk), _F32) + _nbytes((2, tq, seq), _F32)
            + _nbytes((2, tq, seq), _BF16) + 8 * _nbytes((tq, tk), _F32) + (6 << 20))
    return pl.pallas_call(
        functools.partial(_attn_body, tq=tq, tk=tk, d=d, m_lo=m_lo, m_hi=m_hi, lambda_init=lambda_init),
        out_shape=jax.ShapeDtypeStruct((m, heads * w), _BF16),
        grid=(batch, heads, nq),
        in_specs=[
            pl.BlockSpec((4, d), lambda b, h, i: (0, 0)),
            pl.BlockSpec((tq, w), lambda b, h, i: (b * nq + i, h)),
            pl.BlockSpec((seq, w), lambda b, h, i: (b, heads + h)),
            pl.BlockSpec((seq, w), lambda b, h, i: (b, h)),
            pl.BlockSpec((None, nb, tq, tk), lambda b, h, i: (h, 0, 0, 0)),
            pl.BlockSpec((1, w), lambda b, h, i: (0, 0)),
        ],
        out_specs=pl.BlockSpec((tq, w), lambda b, h, i: (b * nq + i, h)),
        scratch_shapes=[pltpu.VMEM((2, tq, seq), _F32), pltpu.VMEM((2, tq, seq), _BF16)],
        compiler_params=_params(("parallel", "parallel", "arbitrary"), vmem),
        name="diff_attention",
    )(lam_vecs, qk, qk, v_src, band, subln_g.reshape(1, w))


def _split_dot(lhs, x, dims=None):
    hi = x.astype(_BF16)
    lo = (x - hi.astype(_F32)).astype(_BF16)
    if dims is None:
        return _dot(lhs, hi) + _dot(lhs, lo)
    return (lax.dot_general(hi, lhs, dims, preferred_element_type=_F32)
            + lax.dot_general(lo, lhs, dims, preferred_element_type=_F32))


def _gla_body(*refs, tt, dk, dv, hpb, reverse, final, q_scale):
    if final:
        (q_ref, k_ref, lr_ref, v_ref, wg_ref, bg_ref, fwd_ref, r_ref, gn_ref, o_ref, s_ref) = refs
    else:
        (q_ref, k_ref, lr_ref, v_ref, wg_ref, bg_ref, o_ref, s_ref) = refs
    c_len = _GLA_CHUNK
    nc = tt // c_len

    @pl.when(pl.program_id(2) == 0)
    def _():
        s_ref[...] = jnp.zeros_like(s_ref)

    row = lax.broadcasted_iota(jnp.int32, (tt, tt), 0)
    col = lax.broadcasted_iota(jnp.int32, (tt, tt), 1)
    same = (row // c_len) == (col // c_len)
    causal = same & ((col >= row) if reverse else (col <= row))
    causal_ones = causal.astype(_BF16)
    same_ones = same.astype(_BF16)
    sel = (lax.broadcasted_iota(jnp.int32, (tt, _LANES), 0) // c_len
           == lax.broadcasted_iota(jnp.int32, (tt, _LANES), 1)).astype(_BF16)
    stems = lr_ref[...].astype(_BF16)

    def chunk_local(hh):
        kc = slice(hh * dk, (hh + 1) * dk)
        v = v_ref[:, hh * dv:(hh + 1) * dv]
        pre = _dot(stems, wg_ref[:, kc]) + bg_ref[:, kc]
        log_a = (jnp.minimum(pre, 0.0) - jnp.log1p(jnp.exp(-jnp.abs(pre)))) / _GLA_TAU
        b = _split_dot(causal_ones, log_a)
        b_tot = _split_dot(same_ones, log_a)
        decay_cols = jnp.exp(_split_dot(sel, log_a, _TN))
        k = k_ref[:, kc]
        q_dec = (q_ref[:, kc] * q_scale * jnp.exp(b)).astype(_BF16)
        k_inv = (k * jnp.exp(-b)).astype(_BF16)
        k_dec = (k * jnp.exp(b_tot - b)).astype(_BF16)
        att = lax.dot_general(q_dec, k_inv, _NT, preferred_element_type=_F32)
        o_intra = _dot(jnp.where(causal, att, 0.0).astype(_BF16), v)
        return v, q_dec, k_dec, decay_cols, o_intra

    local = [chunk_local(hh) for hh in range(hpb)]
    outs = [[None] * nc for _ in range(hpb)]
    for c in (reversed(range(nc)) if reverse else range(nc)):
        rows = slice(c * c_len, (c + 1) * c_len)
        for hh, (v, q_dec, k_dec, decay_cols, o_intra) in enumerate(local):
            s = s_ref[hh]
            outs[hh][c] = o_intra[rows] + _dot(q_dec[rows], s.astype(_BF16))
            s_ref[hh] = decay_cols[:, c:c + 1] * s + lax.dot_general(k_dec[rows], v[rows], _TN,
                                                                    preferred_element_type=_F32)

    for hh in range(hpb):
        vc = slice(hh * dv, (hh + 1) * dv)
        o = jnp.concatenate(outs[hh], axis=0)
        if final:
            o = _rms(o + fwd_ref[:, vc], gn_ref[...])
            r = r_ref[:, vc].astype(_F32)
            o_ref[:, vc] = (o * (r * jax.nn.sigmoid(r))).astype(o_ref.dtype)
        else:
            o_ref[:, vc] = o


def _gla(qk, stems, vgr, w_gate, b_gate, batch, seq, heads, dk, dv, v_col0, reverse,
         fwd=None, r_col0=None, norm_g=None):
    m = qk.shape[0]
    tt = _tile(seq, 256, _GLA_CHUNK)
    nt = seq // tt
    hpb = _tile(heads, 4, 1)
    ng = heads // hpb
    final = fwd is not None
    assert v_col0 % hpb == 0 and (r_col0 is None or r_col0 % hpb == 0)

    def rows(b, t):
        return b * nt + (nt - 1 - t if reverse else t)

    in_specs = [
        pl.BlockSpec((tt, hpb * dk), lambda b, g, t: (rows(b, t), g)),
        pl.BlockSpec((tt, hpb * dk), lambda b, g, t: (rows(b, t), ng + g)),
        pl.BlockSpec((tt, dk), lambda b, g, t: (rows(b, t), 0)),
        pl.BlockSpec((tt, hpb * dv), lambda b, g, t: (rows(b, t), v_col0 // hpb + g)),
        pl.BlockSpec((dk, hpb * dk), lambda b, g, t: (0, g)),
        pl.BlockSpec((1, hpb * dk), lambda b, g, t: (0, g)),
    ]
    args = [qk, qk, stems, vgr, w_gate, b_gate]
    if final:
        in_specs += [
            pl.BlockSpec((tt, hpb * dv), lambda b, g, t: (rows(b, t), g)),
            pl.BlockSpec((tt, hpb * dv), lambda b, g, t: (rows(b, t), r_col0 // hpb + g)),
            pl.BlockSpec((1, dv), lambda b, g, t: (0, 0)),
        ]
        args += [fwd, vgr, norm_g.reshape(1, dv)]
    return pl.pallas_call(
        functools.partial(_gla_body, tt=tt, dk=dk, dv=dv, hpb=hpb, reverse=reverse, final=final,
                          q_scale=dk ** -0.5),
        out_shape=jax.ShapeDtypeStruct((m, heads * dv), _BF16 if final else _F32),
        grid=(batch, ng, nt),
        in_specs=in_specs,
        out_specs=pl.BlockSpec((tt, hpb * dv), lambda b, g, t: (rows(b, t), g)),
        scratch_shapes=[pltpu.VMEM((hpb, dk, dv), _F32)],
        compiler_params=_params(("parallel", "parallel", "arbitrary"), 48 << 20),
        name="gla_bwd" if reverse else "gla_fwd",
    )(*args)


def _merge_body(oa_ref, ob_ref, pa_ref, pb_ref, ga_ref, gb_ref, o_ref):
    a = _dot(oa_ref[...], pa_ref[...])
    b = _dot(ob_ref[...], pb_ref[...])
    o_ref[...] = (ga_ref[...].astype(_F32) * a + gb_ref[...].astype(_F32) * b).astype(o_ref.dtype)


def _merge(o_a, o_b, p_a, p_b, gates):
    m, ka = o_a.shape
    kb = o_b.shape[1]
    n = p_a.shape[1]
    tm = _tile(m, 1024, 16)
    tn = _tile(n, 1024, _LANES)
    nj = n // tn
    vmem = (2 * _nbytes((tm, ka + kb), _BF16) + 2 * _nbytes((ka + kb, tn), _BF16)
            + 4 * _nbytes((tm, tn), _BF16) + 6 * _nbytes((tm, tn), _F32) + (4 << 20))
    return pl.pallas_call(
        _merge_body,
        out_shape=jax.ShapeDtypeStruct((m, n), _BF16),
        grid=(m // tm, nj),
        in_specs=[
            pl.BlockSpec((tm, ka), lambda i, j: (i, 0)),
            pl.BlockSpec((tm, kb), lambda i, j: (i, 0)),
            pl.BlockSpec((ka, tn), lambda i, j: (0, j)),
            pl.BlockSpec((kb, tn), lambda i, j: (0, j)),
            pl.BlockSpec((tm, tn), lambda i, j: (i, j)),
            pl.BlockSpec((tm, tn), lambda i, j: (i, nj + j)),
        ],
        out_specs=pl.BlockSpec((tm, tn), lambda i, j: (i, j)),
        compiler_params=_params(("parallel", "arbitrary"), vmem),
        name="branch_merge",
    )(o_a, o_b, p_a, p_b, gates, gates)


def _gelu_tanh(x):
    return x * (0.5 * (1.0 + jnp.tanh(math.sqrt(2.0 / math.pi) * (x + 0.044715 * (x * x * x)))))


def _ffn_up_body(h_ref, wa_ref, wb_ref, cw_ref, cb_ref, o_ref, *, tm):
    a = _dot(h_ref[...], wa_ref[...])
    g = _dot(h_ref[0:tm, :], wb_ref[...])
    rows = a.shape[0]
    prev = pltpu.roll(a, 1, 0)[0:tm]
    nxt = pltpu.roll(a, rows - 1, 0)[0:tm]
    conv = prev * cw_ref[0:1, :] + a[0:tm] * cw_ref[1:2, :] + nxt * cw_ref[2:3, :] + cb_ref[...]
    o_ref[...] = (_gelu_tanh(conv) * g).astype(o_ref.dtype)


def _ffn_up(h_tiles, w_a, w_b, conv_w, conv_b, tm):
    nt, rows, d = h_tiles.shape
    f = w_a.shape[1]
    tf = _tile(f, 512, _LANES)
    vmem = (2 * _nbytes((rows, d), _BF16) + 4 * _nbytes((d, tf), _BF16) + 2 * _nbytes((tm, tf), _BF16)
            + 8 * _nbytes((rows, tf), _F32) + (4 << 20))
    return pl.pallas_call(
        functools.partial(_ffn_up_body, tm=tm),
        out_shape=jax.ShapeDtypeStruct((nt * tm, f), _BF16),
        grid=(nt, f // tf),
        in_specs=[
            pl.BlockSpec((None, rows, d), lambda i, j: (i, 0, 0)),
            pl.BlockSpec((d, tf), lambda i, j: (0, j)),
            pl.BlockSpec((d, tf), lambda i, j: (0, j)),
            pl.BlockSpec((3, tf), lambda i, j: (0, j)),
            pl.BlockSpec((1, tf), lambda i, j: (0, j)),
        ],
        out_specs=pl.BlockSpec((tm, tf), lambda i, j: (i, j)),
        compiler_params=_params(("parallel", "arbitrary"), vmem),
        name="ffn_up",
    )(h_tiles, w_a, w_b, conv_w, conv_b)


def _ffn_down_body(a_ref, w_ref, r_ref, o_ref):
    @pl.when(pl.program_id(2) == 0)
    def _():
        o_ref[...] = r_ref[...]

    o_ref[...] += _dot(a_ref[...], w_ref[...])


def _ffn_down(act, w, resid):
    m, f = act.shape
    n = w.shape[1]
    tm = _tile(m, 1024, 16)
    tn = _tile(n, 1024, _LANES)
    tk = _tile(f, 2816, _LANES)
    vmem = (2 * _nbytes((tm, tk), _BF16) + 2 * _nbytes((tk, tn), _BF16) + 6 * _nbytes((tm, tn), _F32) + (4 << 20))
    return pl.pallas_call(
        _ffn_down_body,
        out_shape=jax.ShapeDtypeStruct((m, n), _F32),
        grid=(m // tm, n // tn, f // tk),
        in_specs=[
            pl.BlockSpec((tm, tk), lambda i, j, k: (i, k)),
            pl.BlockSpec((tk, tn), lambda i, j, k: (k, j)),
            pl.BlockSpec((tm, tn), lambda i, j, k: (i, j)),
        ],
        out_specs=pl.BlockSpec((tm, tn), lambda i, j, k: (i, j)),
        compiler_params=_params(("parallel", "parallel", "arbitrary"), vmem),
        name="ffn_down",
    )(act, w, resid)


def _pad_cols(w, n):
    return jnp.pad(w, ((0, 0), (0, n - w.shape[1])))


def _prepare_weights(lambda_init, rel_bias, g_mix, w_in, q_norm_g, k_norm_g, lambda_q1, lambda_k1, lambda_q2,
                     lambda_k2, da_subln_g, w_gate_fwd, b_gate_fwd, w_gate_bwd, b_gate_bwd, gla_norm_g,
                     w_branch_a, w_branch_b, w_out, g_ffn, w_up, conv_w, conv_b, w_down):
    heads = rel_bias.shape[1]
    d = q_norm_g.shape[-1]
    da = heads * 2 * d
    rank, gla_k = w_gate_fwd.shape
    dv = gla_norm_g.shape[-1]
    gla_v = w_branch_b.shape[0]
    gh = gla_v // dv
    dk = gla_k // gh
    f = conv_b.shape[-1]
    d_model = w_in.shape[0]
    widths = [da, da, da, gla_k, gla_k, gla_v, gla_v, rank, rank, d_model, d_model]
    starts = [0]
    for wd in widths:
        starts.append(starts[-1] + wd)
    sec = lambda s: (starts[s], widths[s])

    w_stems = _pad_cols(w_in[:, starts[7]:starts[9]].astype(_BF16), dk)
    w_gates = w_in[:, starts[9]:].astype(_BF16)
    qk_gain = jnp.concatenate([jnp.tile(q_norm_g * (d ** -0.5 * _LOG2E), 2 * heads), jnp.tile(k_norm_g, 2 * heads)])

    def gate_weight(w, first_row):
        return jnp.zeros((dk, gla_k), _BF16).at[first_row:first_row + rank].set(w.astype(_BF16))

    fp = -(-f // 1024) * 1024
    zeros = jnp.zeros((d_model, fp - f), _BF16)
    return dict(
        heads=heads, d=d, gh=gh, dk=dk, dv=dv, lambda_init=lambda_init,
        rel_bias=rel_bias, g_mix=g_mix, g_ffn=g_ffn,
        w_in=w_in.astype(_BF16), sec_qk=(sec(0), sec(1)), sec_vgr=(sec(2), sec(5), sec(6)),
        sec_gla_qk=(sec(3), sec(4)),
        w_stems=w_stems, w_gates=w_gates, qk_gain=qk_gain.reshape(1, -1),
        lam_vecs=jnp.stack([lambda_q1, lambda_k1, lambda_q2, lambda_k2]),
        da_subln_g=da_subln_g, gla_norm_g=gla_norm_g,
        wg_fwd=gate_weight(w_gate_fwd, 0), wg_bwd=gate_weight(w_gate_bwd, rank),
        bg_fwd=b_gate_fwd.reshape(1, -1), bg_bwd=b_gate_bwd.reshape(1, -1),
        p_a=w_branch_a.astype(_BF16), p_b=w_branch_b.astype(_BF16), w_out=w_out.astype(_BF16),
        w_up_a=jnp.concatenate([w_up[:, :f].astype(_BF16), zeros], axis=1),
        w_up_b=jnp.concatenate([w_up[:, f:].astype(_BF16), zeros], axis=1),
        conv_w=_pad_cols(conv_w, fp), conv_b=_pad_cols(conv_b.reshape(1, f), fp),
        w_down=jnp.concatenate([w_down.astype(_BF16), zeros.T], axis=0),
    )


def _encoder_layer(x3, p, band, tq, tk):
    batch, seq, d_model = x3.shape
    x = x3.reshape(batch * seq, d_model)
    heads, d, gh, dk, dv = p["heads"], p["d"], p["gh"], p["dk"], p["dv"]

    h = _rmsnorm(x, p["g_mix"])
    qk = _project(functools.partial(_proj_groupnorm_body, group=d), h, p["w_in"], _BF16, p["sec_qk"],
                  extra=(p["qk_gain"],), extra_specs=(_row_vec_spec,), name="proj_qk")
    vgr = _project(_proj_cast_body, h, p["w_in"], _BF16, p["sec_vgr"], name="proj_vgr")
    gla_qk = _project(_proj_cast_body, h, p["w_in"], _F32, p["sec_gla_qk"], name="proj_gla_qk")
    stems = _project(_proj_cast_body, h, p["w_stems"], _F32, name="proj_stems")
    gates = _project(_proj_sigmoid_body, h, p["w_gates"], _BF16, name="proj_gates")

    o_a = _diff_attention(qk, vgr, band, p["lam_vecs"], p["da_subln_g"], batch, seq, heads, d,
                          p["lambda_init"], tq, tk)

    v_col0 = (heads * 2 * d) // dv
    r_col0 = v_col0 + gh
    fwd = _gla(gla_qk, stems, vgr, p["wg_fwd"], p["bg_fwd"], batch, seq, gh, dk, dv, v_col0, reverse=False)
    o_b = _gla(gla_qk, stems, vgr, p["wg_bwd"], p["bg_bwd"], batch, seq, gh, dk, dv, v_col0, reverse=True,
               fwd=fwd, r_col0=r_col0, norm_g=p["gla_norm_g"])

    merged = _merge(o_a, o_b, p["p_a"], p["p_b"], gates)
    x1 = _project(_proj_residual_body, merged, p["w_out"], _F32, extra=(x,), extra_specs=(_tile_spec,),
                  name="proj_out")

    tm = _tile(seq, 1024, 2 * _HALO)
    h2 = _rmsnorm_halo(x1, p["g_ffn"], seq, tm)
    act = _ffn_up(h2, p["w_up_a"], p["w_up_b"], p["conv_w"], p["conv_b"], tm)
    y = _ffn_down(act, p["w_down"], x1)
    return y.reshape(batch, seq, d_model)


def kernel(x_prompt, x_sample, rel_bias, g_mix, w_in, q_norm_g, k_norm_g, lambda_q1, lambda_k1, lambda_q2,
           lambda_k2, da_subln_g, w_gate_fwd, b_gate_fwd, w_gate_bwd, b_gate_bwd, gla_norm_g, w_branch_a,
           w_branch_b, w_out, g_ffn, w_up, conv_w, conv_b, w_down):
    layer_weights = (g_mix, w_in, q_norm_g, k_norm_g, lambda_q1, lambda_k1, lambda_q2, lambda_k2, da_subln_g,
                     w_gate_fwd, b_gate_fwd, w_gate_bwd, b_gate_bwd, gla_norm_g, w_branch_a, w_branch_b, w_out,
                     g_ffn, w_up, conv_w, conv_b, w_down)
    min_seq = min(x_prompt.shape[1], x_sample.shape[1])
    tq = _tile(min_seq, 512, _LANES)
    tk = _tile(min_seq, 512, tq)
    band = _bias_band(rel_bias, tq, tk)
    y_prompt, y_sample = x_prompt, x_sample
    for l in range(g_mix.shape[0]):
        lambda_init = 0.8 - 0.6 * math.exp(-0.3 * l)
        p = _prepare_weights(lambda_init, rel_bias, *(w[l] for w in layer_weights))
        y_prompt = _encoder_layer(y_prompt, p, band, tq, tk)
        y_sample = _encoder_layer(y_sample, p, band, tq, tk)
    return (y_prompt, y_sample)
```

```python
import functools
import math

import jax
import jax.numpy as jnp
from jax import lax
from jax.experimental import pallas as pl
from jax.experimental.pallas import tpu as pltpu

_EPS = 1e-6
_GLA_CHUNK = 64
_GLA_TAU = 16.0
_REL_MAX_DIST = 128
_LOG2E = math.log2(math.e)
_LANES = 128
_HALO = 16
_ROW_GROUP = 16
_FFN_TILE = 512
_V7X_VMEM_BYTES = 64 * 1024 * 1024
_VMEM_BUDGET = _V7X_VMEM_BYTES - 8 * 1024 * 1024

_F32 = jnp.float32
_BF16 = jnp.bfloat16
_NT = (((1,), (1,)), ((), ()))
_TN = (((0,), (0,)), ((), ()))


def _params(semantics, vmem_bytes):
    return pltpu.CompilerParams(dimension_semantics=semantics,
                                vmem_limit_bytes=int(min(vmem_bytes, _VMEM_BUDGET)))


def _tile(n, pref, align):
    t = min(pref, n)
    t -= t % align
    while t > align and n % t:
        t -= align
    assert t >= align and n % t == 0, (n, pref, align)
    return t


def _nbytes(shape, dtype):
    return math.prod(shape) * jnp.dtype(dtype).itemsize


def _dot(a, b):
    return jnp.dot(a, b, preferred_element_type=_F32)


def _rms(x, g):
    ms = jnp.mean(x * x, axis=-1, keepdims=True)
    return x * lax.rsqrt(ms + _EPS) * g


def _rmsnorm_body(x_ref, g_ref, o_ref):
    o_ref[...] = _rms(x_ref[...], g_ref[...]).astype(o_ref.dtype)


def _rmsnorm(x, g):
    m, d = x.shape
    tm = _tile(m, 256, 8)
    return pl.pallas_call(
        _rmsnorm_body,
        out_shape=jax.ShapeDtypeStruct((m, d), _BF16),
        grid=(m // tm,),
        in_specs=[pl.BlockSpec((tm, d), lambda i: (i, 0)), pl.BlockSpec((1, d), lambda i: (0, 0))],
        out_specs=pl.BlockSpec((tm, d), lambda i: (i, 0)),
        compiler_params=_params(("parallel",), 6 * _nbytes((tm, d), _F32) + (8 << 20)),
        name="rmsnorm",
    )(x, g.reshape(1, d))


def _rmsnorm_halo_body(x_ref, prev_ref, next_ref, g_ref, o_ref, *, tm, tr, tiles_per_seq):
    r = pl.program_id(1)
    row0 = pl.multiple_of(r * tr, tr)
    o_ref[pl.ds(row0, tr), :] = _rms(x_ref[...], g_ref[...]).astype(o_ref.dtype)

    @pl.when(r == 0)
    def _():
        pos = pl.program_id(0) % tiles_per_seq
        nxt = jnp.where(pos == tiles_per_seq - 1, 0.0, _rms(next_ref[...], g_ref[...]))
        prv = jnp.where(pos == 0, 0.0, _rms(prev_ref[...], g_ref[...]))
        o_ref[tm:tm + _HALO, :] = nxt.astype(o_ref.dtype)
        o_ref[tm + _HALO:tm + 2 * _HALO, :] = prv.astype(o_ref.dtype)


def _rmsnorm_halo(x, g, seq, tm):
    m, d = x.shape
    tr = _tile(tm, 256, _HALO)
    nt, nr = m // tm, tm // tr
    hb = tm // _HALO
    last = m // _HALO - 1
    return pl.pallas_call(
        functools.partial(_rmsnorm_halo_body, tm=tm, tr=tr, tiles_per_seq=seq // tm),
        out_shape=jax.ShapeDtypeStruct((nt, tm + 2 * _HALO, d), _BF16),
        grid=(nt, nr),
        in_specs=[
            pl.BlockSpec((tr, d), lambda i, r: (i * nr + r, 0)),
            pl.BlockSpec((_HALO, d), lambda i, r: (jnp.maximum(i * hb - 1, 0), 0)),
            pl.BlockSpec((_HALO, d), lambda i, r: (jnp.minimum((i + 1) * hb, last), 0)),
            pl.BlockSpec((1, d), lambda i, r: (0, 0)),
        ],
        out_specs=pl.BlockSpec((None, tm + 2 * _HALO, d), lambda i, r: (i, 0, 0)),
        compiler_params=_params(("parallel", "arbitrary"),
                                6 * _nbytes((tr, d), _F32) + 2 * _nbytes((tm + 2 * _HALO, d), _BF16) + (8 << 20)),
        name="rmsnorm_halo",
    )(x, x, x, g.reshape(1, d))


def _proj_cast_body(a_ref, b_ref, o_ref):
    o_ref[...] = _dot(a_ref[...], b_ref[...]).astype(o_ref.dtype)


def _proj_sigmoid_body(a_ref, b_ref, o_ref):
    o_ref[...] = jax.nn.sigmoid(_dot(a_ref[...], b_ref[...])).astype(o_ref.dtype)


def _proj_groupnorm_body(a_ref, b_ref, g_ref, o_ref, *, group):
    acc = _dot(a_ref[...], b_ref[...])
    for c in range(acc.shape[1] // group):
        cols = slice(c * group, (c + 1) * group)
        o_ref[:, cols] = _rms(acc[:, cols], g_ref[:, cols]).astype(o_ref.dtype)


def _proj_residual_body(a_ref, b_ref, r_ref, o_ref):
    o_ref[...] = r_ref[...] + _dot(a_ref[...], b_ref[...])


def _project(body, a, b, out_dtype, sections=None, extra=(), extra_specs=(), tile=1024, name="project"):
    m, k = a.shape
    sections = sections or ((0, b.shape[1]),)
    n = sum(width for _, width in sections)
    tm = _tile(m, tile, 16)
    tn = _tile(math.gcd(*(v for sec in sections for v in sec)), tile, _LANES)
    first_out, shifts, done = [], [], 0
    for start, width in sections:
        first_out.append(done // tn)
        shifts.append(start // tn - done // tn)
        done += width

    def src_block(j):
        shift = shifts[0]
        for first, s in zip(first_out[1:], shifts[1:]):
            shift = jnp.where(j >= first, s, shift)
        return j + shift

    vmem = (2 * _nbytes((tm, k), a.dtype) + 2 * _nbytes((k, tn), b.dtype)
            + 6 * _nbytes((tm, tn), _F32) + (4 << 20))
    return pl.pallas_call(
        body,
        out_shape=jax.ShapeDtypeStruct((m, n), out_dtype),
        grid=(m // tm, n // tn),
        in_specs=[pl.BlockSpec((tm, k), lambda i, j: (i, 0)),
                  pl.BlockSpec((k, tn), lambda i, j: (0, src_block(j))),
                  *[spec(tm, tn) for spec in extra_specs]],
        out_specs=pl.BlockSpec((tm, tn), lambda i, j: (i, j)),
        compiler_params=_params(("parallel", "arbitrary"), vmem),
        name=name,
    )(a, b, *extra)


def _row_vec_spec(tm, tn):
    return pl.BlockSpec((1, tn), lambda i, j: (0, j))


def _tile_spec(tm, tn):
    return pl.BlockSpec((tm, tn), lambda i, j: (i, j))


def _t5_bucket(rel, num_buckets):
    half = num_buckets // 2
    max_exact = half // 2
    ret = jnp.where(rel > 0, half, 0)
    n = jnp.abs(rel)
    nf = jnp.maximum(n, 1).astype(_F32)
    large = max_exact + (jnp.log(nf / max_exact) / math.log(_REL_MAX_DIST / max_exact)
                         * (half - max_exact)).astype(jnp.int32)
    large = jnp.minimum(large, half - 1)
    return ret + jnp.where(n < max_exact, n, large)


def _band_body(rb_ref, o_ref, *, tq, tk, m_lo, num_buckets):
    h = pl.program_id(0)
    m = pl.program_id(1) + m_lo
    row = lax.broadcasted_iota(jnp.int32, (tq, tk), 0)
    col = lax.broadcasted_iota(jnp.int32, (tq, tk), 1)
    bucket = _t5_bucket(m * tq + col - row, num_buckets)
    out = jnp.zeros((tq, tk), _F32)
    for b in range(num_buckets):
        out = jnp.where(bucket == b, rb_ref[h, b], out)
    o_ref[...] = out * _LOG2E


def _band_range(tq, tk):
    m_lo = -((_REL_MAX_DIST - 1 + tk + tq - 1) // tq)
    m_hi = (_REL_MAX_DIST - 1 + tq + tq - 1) // tq
    return m_lo, m_hi


def _bias_band(rel_bias, tq, tk):
    nbk, h = rel_bias.shape
    m_lo, m_hi = _band_range(tq, tk)
    nb = m_hi - m_lo + 1
    return pl.pallas_call(
        functools.partial(_band_body, tq=tq, tk=tk, m_lo=m_lo, num_buckets=nbk),
        out_shape=jax.ShapeDtypeStruct((h, nb, tq, tk), _F32),
        grid=(h, nb),
        in_specs=[pl.BlockSpec(memory_space=pltpu.SMEM)],
        out_specs=pl.BlockSpec((None, None, tq, tk), lambda i, j: (i, j, 0, 0)),
        compiler_params=_params(("parallel", "parallel"), 24 * _nbytes((tq, tk), _F32) + (4 << 20)),
        name="bias_band",
    )(rel_bias.T)


def _lane_fold(x, op):
    return functools.reduce(op, [x[:, c:c + _LANES] for c in range(0, x.shape[1], _LANES)])


def _attn_body(lam_ref, q_ref, k_ref, v_ref, band_ref, g_ref, o_ref, s_ref, p_ref,
               *, tq, tk, d, m_lo, m_hi, lambda_init):
    i = pl.program_id(2)
    nk = k_ref.shape[0] // tk
    ratio = tk // tq

    band_idx = [jnp.clip(j * ratio - i, m_lo, m_hi) - m_lo for j in range(nk)]

    def scores(mp):
        cols = slice(mp * d, (mp + 1) * d)
        qm = q_ref[:, cols]
        for j in range(nk):
            keys = slice(j * tk, (j + 1) * tk)
            s_ref[mp, :, keys] = lax.dot_general(qm, k_ref[keys, cols], _NT, preferred_element_type=_F32)

    def softmax_rows(mp):
        sums = []
        for r in range(0, tq, _ROW_GROUP):
            rows = slice(r, r + _ROW_GROUP)
            part_max = None
            for j in range(nk):
                keys = slice(j * tk, (j + 1) * tk)
                folded = _lane_fold(s_ref[mp, rows, keys] + band_ref[band_idx[j], rows, :], jnp.maximum)
                part_max = folded if part_max is None else jnp.maximum(part_max, folded)
            row_max = jnp.max(part_max, axis=-1, keepdims=True)
            part_sum = jnp.zeros((_ROW_GROUP, _LANES), _F32)
            for j in range(nk):
                keys = slice(j * tk, (j + 1) * tk)
                p = jnp.exp2(s_ref[mp, rows, keys] + (band_ref[band_idx[j], rows, :] - row_max))
                part_sum = part_sum + _lane_fold(p, jnp.add)
                p_ref[mp, rows, keys] = p.astype(p_ref.dtype)
            sums.append(jnp.sum(part_sum, axis=-1, keepdims=True))
        return jnp.concatenate(sums, axis=0)

    scores(0)
    scores(1)
    heads_out = []
    for mp in range(2):
        row_sum = softmax_rows(mp)
        heads_out.append(_dot(p_ref[mp], v_ref[...]) / row_sum)

    lv = lam_ref[...]
    lam = (jnp.exp(jnp.sum(lv[0:1] * lv[1:2], axis=-1, keepdims=True))
           - jnp.exp(jnp.sum(lv[2:3] * lv[3:4], axis=-1, keepdims=True)) + lambda_init)
    o = heads_out[0] - lam * heads_out[1]
    o_ref[...] = (_rms(o, g_ref[...]) * (1.0 - lambda_init)).astype(o_ref.dtype)


def _diff_attention(qk, v_src, band, lam_vecs, subln_g, batch, seq, heads, d, lambda_init, tq, tk):
    m = qk.shape[0]
    nq = seq // tq
    m_lo, m_hi = _band_range(tq, tk)
    nb = m_hi - m_lo + 1
    w = 2 * d
    vmem = (4 * _nbytes((seq, w), _BF16) + 2 * _nbytes((nb, tq, tk), _F32) + _nbytes((2, tq, seq), _F32)
            + _nbytes((2, tq, seq), _BF16) + 8 * _nbytes((tq, tk), _F32) + (6 << 20))
    return pl.pallas_call(
        functools.partial(_attn_body, tq=tq, tk=tk, d=d, m_lo=m_lo, m_hi=m_hi, lambda_init=lambda_init),
        out_shape=jax.ShapeDtypeStruct((m, heads * w), _BF16),
        grid=(batch, heads, nq),
        in_specs=[
            pl.BlockSpec((4, d), lambda b, h, i: (0, 0)),
            pl.BlockSpec((tq, w), lambda b, h, i: (b * nq + i, h)),
            pl.BlockSpec((seq, w), lambda b, h, i: (b, heads + h)),
            pl.BlockSpec((seq, w), lambda b, h, i: (b, h)),
            pl.BlockSpec((None, nb, tq, tk), lambda b, h, i: (h, 0, 0, 0)),
            pl.BlockSpec((1, w), lambda b, h, i: (0, 0)),
        ],
        out_specs=pl.BlockSpec((tq, w), lambda b, h, i: (b * nq + i, h)),
        scratch_shapes=[pltpu.VMEM((2, tq, seq), _F32), pltpu.VMEM((2, tq, seq), _BF16)],
        compiler_params=_params(("parallel", "parallel", "arbitrary"), vmem),
        name="diff_attention",
    )(lam_vecs, qk, qk, v_src, band, subln_g.reshape(1, w))


def _split_dot(lhs, x, dims=None):
    hi = x.astype(_BF16)
    lo = (x - hi.astype(_F32)).astype(_BF16)
    if dims is None:
        return _dot(lhs, hi) + _dot(lhs, lo)
    return (lax.dot_general(hi, lhs, dims, preferred_element_type=_F32)
            + lax.dot_general(lo, lhs, dims, preferred_element_type=_F32))


def _gla_body(*refs, tt, dk, dv, hpb, reverse, final, q_scale):
    if final:
        (q_ref, k_ref, lr_ref, v_ref, wg_ref, bg_ref, fwd_ref, r_ref, gn_ref, o_ref, s_ref) = refs
    else:
        (q_ref, k_ref, lr_ref, v_ref, wg_ref, bg_ref, o_ref, s_ref) = refs
    c_len = _GLA_CHUNK
    nc = tt // c_len

    @pl.when(pl.program_id(2) == 0)
    def _():
        s_ref[...] = jnp.zeros_like(s_ref)

    row = lax.broadcasted_iota(jnp.int32, (tt, tt), 0)
    col = lax.broadcasted_iota(jnp.int32, (tt, tt), 1)
    same = (row // c_len) == (col // c_len)
    causal = same & ((col >= row) if reverse else (col <= row))
    causal_ones = causal.astype(_BF16)
    edge = 0 if reverse else c_len - 1
    sel = (lax.broadcasted_iota(jnp.int32, (tt, _LANES), 0) // c_len
           == lax.broadcasted_iota(jnp.int32, (tt, _LANES), 1)).astype(_BF16)
    stems = lr_ref[...].astype(_BF16)

    def chunk_local(hh):
        kc = slice(hh * dk, (hh + 1) * dk)
        v = v_ref[:, hh * dv:(hh + 1) * dv]
        pre = _dot(stems, wg_ref[:, kc]) + bg_ref[:, kc]
        log_a = (jnp.minimum(pre, 0.0) - jnp.log(1.0 + jnp.exp(-jnp.abs(pre)))) / _GLA_TAU
        b = _split_dot(causal_ones, log_a)
        b_tot = jnp.concatenate([jnp.broadcast_to(b[c * c_len + edge:c * c_len + edge + 1], (c_len, dk))
                                 for c in range(nc)], axis=0)
        decay_cols = jnp.exp(_split_dot(sel, log_a, _TN))
        k = k_ref[:, kc]
        q_dec = (q_ref[:, kc] * q_scale * jnp.exp(b)).astype(_BF16)
        k_inv = (k * jnp.exp(-b)).astype(_BF16)
        k_dec = (k * jnp.exp(b_tot - b)).astype(_BF16)
        att = lax.dot_general(q_dec, k_inv, _NT, preferred_element_type=_F32)
        o_intra = _dot(jnp.where(causal, att, 0.0).astype(_BF16), v)
        return v, q_dec, k_dec, decay_cols, o_intra

    local = [chunk_local(hh) for hh in range(hpb)]
    outs = [[None] * nc for _ in range(hpb)]
    for c in (reversed(range(nc)) if reverse else range(nc)):
        rows = slice(c * c_len, (c + 1) * c_len)
        for hh, (v, q_dec, k_dec, decay_cols, o_intra) in enumerate(local):
            s = s_ref[hh]
            outs[hh][c] = o_intra[rows] + _dot(q_dec[rows], s.astype(_BF16))
            s_ref[hh] = decay_cols[:, c:c + 1] * s + lax.dot_general(k_dec[rows], v[rows], _TN,
                                                                    preferred_element_type=_F32)

    for hh in range(hpb):
        vc = slice(hh * dv, (hh + 1) * dv)
        o = jnp.concatenate(outs[hh], axis=0)
        if final:
            o = _rms(o + fwd_ref[:, vc], gn_ref[...])
            r = r_ref[:, vc].astype(_F32)
            o_ref[:, vc] = (o * (r * jax.nn.sigmoid(r))).astype(o_ref.dtype)
        else:
            o_ref[:, vc] = o


def _gla(qk, stems, vgr, w_gate, b_gate, batch, seq, heads, dk, dv, v_col0, reverse,
         fwd=None, r_col0=None, norm_g=None):
    m = qk.shape[0]
    tt = _tile(seq, 256, _GLA_CHUNK)
    nt = seq // tt
    hpb = _tile(heads, 4, 1)
    ng = heads // hpb
    final = fwd is not None
    assert v_col0 % hpb == 0 and (r_col0 is None or r_col0 % hpb == 0)

    def rows(b, t):
        return b * nt + (nt - 1 - t if reverse else t)

    in_specs = [
        pl.BlockSpec((tt, hpb * dk), lambda b, g, t: (rows(b, t), g)),
        pl.BlockSpec((tt, hpb * dk), lambda b, g, t: (rows(b, t), ng + g)),
        pl.BlockSpec((tt, dk), lambda b, g, t: (rows(b, t), 0)),
        pl.BlockSpec((tt, hpb * dv), lambda b, g, t: (rows(b, t), v_col0 // hpb + g)),
        pl.BlockSpec((dk, hpb * dk), lambda b, g, t: (0, g)),
        pl.BlockSpec((1, hpb * dk), lambda b, g, t: (0, g)),
    ]
    args = [qk, qk, stems, vgr, w_gate, b_gate]
    if final:
        in_specs += [
            pl.BlockSpec((tt, hpb * dv), lambda b, g, t: (rows(b, t), g)),
            pl.BlockSpec((tt, hpb * dv), lambda b, g, t: (rows(b, t), r_col0 // hpb + g)),
            pl.BlockSpec((1, dv), lambda b, g, t: (0, 0)),
        ]
        args += [fwd, vgr, norm_g.reshape(1, dv)]
    return pl.pallas_call(
        functools.partial(_gla_body, tt=tt, dk=dk, dv=dv, hpb=hpb, reverse=reverse, final=final,
                          q_scale=dk ** -0.5),
        out_shape=jax.ShapeDtypeStruct((m, heads * dv), _BF16 if final else _F32),
        grid=(batch, ng, nt),
        in_specs=in_specs,
        out_specs=pl.BlockSpec((tt, hpb * dv), lambda b, g, t: (rows(b, t), g)),
        scratch_shapes=[pltpu.VMEM((hpb, dk, dv), _F32)],
        compiler_params=_params(("parallel", "parallel", "arbitrary"), 48 << 20),
        name="gla_bwd" if reverse else "gla_fwd",
    )(*args)


def _merge_body(oa_ref, ob_ref, pa_ref, pb_ref, ga_ref, gb_ref, o_ref):
    a = _dot(oa_ref[...], pa_ref[...])
    b = _dot(ob_ref[...], pb_ref[...])
    o_ref[...] = (ga_ref[...].astype(_F32) * a + gb_ref[...].astype(_F32) * b).astype(o_ref.dtype)


def _merge(o_a, o_b, p_a, p_b, gates):
    m, ka = o_a.shape
    kb = o_b.shape[1]
    n = p_a.shape[1]
    tm = _tile(m, 1024, 16)
    tn = _tile(n, 1024, _LANES)
    nj = n // tn
    vmem = (2 * _nbytes((tm, ka + kb), _BF16) + 2 * _nbytes((ka + kb, tn), _BF16)
            + 4 * _nbytes((tm, tn), _BF16) + 6 * _nbytes((tm, tn), _F32) + (4 << 20))
    return pl.pallas_call(
        _merge_body,
        out_shape=jax.ShapeDtypeStruct((m, n), _BF16),
        grid=(m // tm, nj),
        in_specs=[
            pl.BlockSpec((tm, ka), lambda i, j: (i, 0)),
            pl.BlockSpec((tm, kb), lambda i, j: (i, 0)),
            pl.BlockSpec((ka, tn), lambda i, j: (0, j)),
            pl.BlockSpec((kb, tn), lambda i, j: (0, j)),
            pl.BlockSpec((tm, tn), lambda i, j: (i, j)),
            pl.BlockSpec((tm, tn), lambda i, j: (i, nj + j)),
        ],
        out_specs=pl.BlockSpec((tm, tn), lambda i, j: (i, j)),
        compiler_params=_params(("parallel", "arbitrary"), vmem),
        name="branch_merge",
    )(o_a, o_b, p_a, p_b, gates, gates)


def _gelu_tanh(x):
    return x * (0.5 * (1.0 + jnp.tanh(math.sqrt(2.0 / math.pi) * (x + 0.044715 * (x * x * x)))))


def _ffn_up_weight_body(w_ref, o_ref, *, nblk):
    t = pl.program_id(0)
    q = 2 * (t // 4) + t % 2
    o_ref[...] = jnp.where(q < nblk, w_ref[...], 0.0).astype(o_ref.dtype)


def _ffn_up_weight(w_up, f):
    d = w_up.shape[0]
    blk = _FFN_TILE // 2
    assert f % blk == 0
    nblk = f // blk
    n_out = 4 * (-(-f // _FFN_TILE))

    def src(t):
        q = 2 * (t // 4) + t % 2
        return 0, jnp.minimum(jnp.where(t % 4 < 2, q, nblk + q), 2 * nblk - 1)

    return pl.pallas_call(
        functools.partial(_ffn_up_weight_body, nblk=nblk),
        out_shape=jax.ShapeDtypeStruct((d, n_out * blk), _BF16),
        grid=(n_out,),
        in_specs=[pl.BlockSpec((d, blk), src)],
        out_specs=pl.BlockSpec((d, blk), lambda t: (0, t)),
        compiler_params=_params(("parallel",), 6 * _nbytes((d, blk), _F32) + (4 << 20)),
        name="ffn_up_weight",
    )(w_up)


def _ffn_up_body(h_ref, wa_ref, wb_ref, cw_ref, cb_ref, o_ref, *, tm):
    a = _dot(h_ref[...], wa_ref[...])
    g = _dot(h_ref[0:tm, :], wb_ref[...])
    rows = a.shape[0]
    prev = pltpu.roll(a, 1, 0)[0:tm]
    nxt = pltpu.roll(a, rows - 1, 0)[0:tm]
    conv = prev * cw_ref[0:1, :] + a[0:tm] * cw_ref[1:2, :] + nxt * cw_ref[2:3, :] + cb_ref[...]
    o_ref[...] = (_gelu_tanh(conv) * g).astype(o_ref.dtype)


def _ffn_up(h_tiles, w_ag, conv_w, conv_b, tm, f):
    nt, rows, d = h_tiles.shape
    tf = _FFN_TILE
    vmem = (2 * _nbytes((rows, d), _BF16) + 4 * _nbytes((d, tf), _BF16) + 2 * _nbytes((tm, tf), _BF16)
            + 8 * _nbytes((rows, tf), _F32) + (4 << 20))
    return pl.pallas_call(
        functools.partial(_ffn_up_body, tm=tm),
        out_shape=jax.ShapeDtypeStruct((nt * tm, f), _BF16),
        grid=(nt, w_ag.shape[1] // (2 * tf)),
        in_specs=[
            pl.BlockSpec((None, rows, d), lambda i, j: (i, 0, 0)),
            pl.BlockSpec((d, tf), lambda i, j: (0, 2 * j)),
            pl.BlockSpec((d, tf), lambda i, j: (0, 2 * j + 1)),
            pl.BlockSpec((3, tf), lambda i, j: (0, j)),
            pl.BlockSpec((1, tf), lambda i, j: (0, j)),
        ],
        out_specs=pl.BlockSpec((tm, tf), lambda i, j: (i, j)),
        compiler_params=_params(("parallel", "arbitrary"), vmem),
        name="ffn_up",
    )(h_tiles, w_ag, w_ag, conv_w, conv_b)


def _pad_cols(w, n):
    return jnp.pad(w, ((0, 0), (0, n - w.shape[1])))


def _prepare_weights(lambda_init, rel_bias, g_mix, w_in, q_norm_g, k_norm_g, lambda_q1, lambda_k1, lambda_q2,
                     lambda_k2, da_subln_g, w_gate_fwd, b_gate_fwd, w_gate_bwd, b_gate_bwd, gla_norm_g,
                     w_branch_a, w_branch_b, w_out, g_ffn, w_up, conv_w, conv_b, w_down):
    heads = rel_bias.shape[1]
    d = q_norm_g.shape[-1]
    da = heads * 2 * d
    rank, gla_k = w_gate_fwd.shape
    dv = gla_norm_g.shape[-1]
    gla_v = w_branch_b.shape[0]
    gh = gla_v // dv
    dk = gla_k // gh
    f = conv_b.shape[-1]
    d_model = w_in.shape[0]
    widths = [da, da, da, gla_k, gla_k, gla_v, gla_v, rank, rank, d_model, d_model]
    starts = [0]
    for wd in widths:
        starts.append(starts[-1] + wd)
    sec = lambda s: (starts[s], widths[s])

    w_stems = _pad_cols(w_in[:, starts[7]:starts[9]].astype(_BF16), dk)
    w_gates = w_in[:, starts[9]:].astype(_BF16)
    qk_gain = jnp.concatenate([jnp.tile(q_norm_g * (d ** -0.5 * _LOG2E), 2 * heads), jnp.tile(k_norm_g, 2 * heads)])

    def gate_weight(w, first_row):
        return jnp.zeros((dk, gla_k), _BF16).at[first_row:first_row + rank].set(w.astype(_BF16))

    fp = -(-f // _FFN_TILE) * _FFN_TILE
    return dict(
        heads=heads, d=d, gh=gh, dk=dk, dv=dv, lambda_init=lambda_init,
        rel_bias=rel_bias, g_mix=g_mix, g_ffn=g_ffn,
        w_in=w_in.astype(_BF16), sec_qk=(sec(0), sec(1)), sec_vgr=(sec(2), sec(5), sec(6)),
        sec_gla_qk=(sec(3), sec(4)),
        w_stems=w_stems, w_gates=w_gates, qk_gain=qk_gain.reshape(1, -1),
        lam_vecs=jnp.stack([lambda_q1, lambda_k1, lambda_q2, lambda_k2]),
        da_subln_g=da_subln_g, gla_norm_g=gla_norm_g,
        wg_fwd=gate_weight(w_gate_fwd, 0), wg_bwd=gate_weight(w_gate_bwd, rank),
        bg_fwd=b_gate_fwd.reshape(1, -1), bg_bwd=b_gate_bwd.reshape(1, -1),
        p_a=w_branch_a.astype(_BF16), p_b=w_branch_b.astype(_BF16), w_out=w_out.astype(_BF16),
        f=f, w_up=_ffn_up_weight(w_up, f),
        conv_w=_pad_cols(conv_w, fp), conv_b=_pad_cols(conv_b.reshape(1, f), fp),
        w_down=w_down.astype(_BF16),
    )


def _encoder_layer(x3, p, band, tq, tk):
    batch, seq, d_model = x3.shape
    x = x3.reshape(batch * seq, d_model)
    heads, d, gh, dk, dv = p["heads"], p["d"], p["gh"], p["dk"], p["dv"]

    h = _rmsnorm(x, p["g_mix"])
    qk = _project(functools.partial(_proj_groupnorm_body, group=d), h, p["w_in"], _BF16, p["sec_qk"],
                  extra=(p["qk_gain"],), extra_specs=(_row_vec_spec,), name="proj_qk")
    vgr = _project(_proj_cast_body, h, p["w_in"], _BF16, p["sec_vgr"], name="proj_vgr")
    gla_qk = _project(_proj_cast_body, h, p["w_in"], _F32, p["sec_gla_qk"], name="proj_gla_qk")
    stems = _project(_proj_cast_body, h, p["w_stems"], _F32, name="proj_stems")
    gates = _project(_proj_sigmoid_body, h, p["w_gates"], _BF16, name="proj_gates")

    o_a = _diff_attention(qk, vgr, band, p["lam_vecs"], p["da_subln_g"], batch, seq, heads, d,
                          p["lambda_init"], tq, tk)

    v_col0 = (heads * 2 * d) // dv
    r_col0 = v_col0 + gh
    fwd = _gla(gla_qk, stems, vgr, p["wg_fwd"], p["bg_fwd"], batch, seq, gh, dk, dv, v_col0, reverse=False)
    o_b = _gla(gla_qk, stems, vgr, p["wg_bwd"], p["bg_bwd"], batch, seq, gh, dk, dv, v_col0, reverse=True,
               fwd=fwd, r_col0=r_col0, norm_g=p["gla_norm_g"])

    merged = _merge(o_a, o_b, p["p_a"], p["p_b"], gates)
    x1 = _project(_proj_residual_body, merged, p["w_out"], _F32, extra=(x,), extra_specs=(_tile_spec,),
                  name="proj_out")

    tm = _tile(seq, 1024, 2 * _HALO)
    h2 = _rmsnorm_halo(x1, p["g_ffn"], seq, tm)
    act = _ffn_up(h2, p["w_up"], p["conv_w"], p["conv_b"], tm, p["f"])
    y = _project(_proj_residual_body, act, p["w_down"], _F32, extra=(x1,), extra_specs=(_tile_spec,),
                 tile=512, name="ffn_down")
    return y.reshape(batch, seq, d_model)


def kernel(x_prompt, x_sample, rel_bias, g_mix, w_in, q_norm_g, k_norm_g, lambda_q1, lambda_k1, lambda_q2,
           lambda_k2, da_subln_g, w_gate_fwd, b_gate_fwd, w_gate_bwd, b_gate_bwd, gla_norm_g, w_branch_a,
           w_branch_b, w_out, g_ffn, w_up, conv_w, conv_b, w_down):
    layer_weights = (g_mix, w_in, q_norm_g, k_norm_g, lambda_q1, lambda_k1, lambda_q2, lambda_k2, da_subln_g,
                     w_gate_fwd, b_gate_fwd, w_gate_bwd, b_gate_bwd, gla_norm_g, w_branch_a, w_branch_b, w_out,
                     g_ffn, w_up, conv_w, conv_b, w_down)
    min_seq = min(x_prompt.shape[1], x_sample.shape[1])
    tq = _tile(min_seq, 512, _LANES)
    tk = _tile(min_seq, 512, tq)
    band = _bias_band(rel_bias, tq, tk)
    y_prompt, y_sample = x_prompt, x_sample
    for l in range(g_mix.shape[0]):
        lambda_init = 0.8 - 0.6 * math.exp(-0.3 * l)
        p = _prepare_weights(lambda_init, rel_bias, *(w[l] for w in layer_weights))
        y_prompt = _encoder_layer(y_prompt, p, band, tq, tk)
        y_sample = _encoder_layer(y_sample, p, band, tq, tk)
    return (y_prompt, y_sample)
```

```python
import functools
import math

import jax
import jax.numpy as jnp
from jax import lax
from jax.experimental import pallas as pl
from jax.experimental.pallas import tpu as pltpu

_EPS = 1e-6
_GLA_CHUNK = 64
_GLA_TAU = 16.0
_REL_MAX_DIST = 128
_LOG2E = math.log2(math.e)
_LANES = 128
_HALO = 16
_ROW_GROUP = 16
_FFN_TILE = 512
_V7X_VMEM_BYTES = 64 * 1024 * 1024
_VMEM_BUDGET = _V7X_VMEM_BYTES - 8 * 1024 * 1024

_F32 = jnp.float32
_BF16 = jnp.bfloat16
_NT = (((1,), (1,)), ((), ()))
_TN = (((0,), (0,)), ((), ()))


def _params(semantics, vmem_bytes):
    return pltpu.CompilerParams(dimension_semantics=semantics,
                                vmem_limit_bytes=int(min(vmem_bytes, _VMEM_BUDGET)))


def _tile(n, pref, align):
    t = min(pref, n)
    t -= t % align
    while t > align and n % t:
        t -= align
    assert t >= align and n % t == 0, (n, pref, align)
    return t


def _nbytes(shape, dtype):
    return math.prod(shape) * jnp.dtype(dtype).itemsize


def _dot(a, b):
    return jnp.dot(a, b, preferred_element_type=_F32)


def _rms(x, g):
    ms = jnp.mean(x * x, axis=-1, keepdims=True)
    return x * lax.rsqrt(ms + _EPS) * g


def _rmsnorm_body(x_ref, g_ref, o_ref):
    o_ref[...] = _rms(x_ref[...], g_ref[...]).astype(o_ref.dtype)


def _rmsnorm(x, g):
    m, d = x.shape
    tm = _tile(m, 256, 8)
    return pl.pallas_call(
        _rmsnorm_body,
        out_shape=jax.ShapeDtypeStruct((m, d), _BF16),
        grid=(m // tm,),
        in_specs=[pl.BlockSpec((tm, d), lambda i: (i, 0)), pl.BlockSpec((1, d), lambda i: (0, 0))],
        out_specs=pl.BlockSpec((tm, d), lambda i: (i, 0)),
        compiler_params=_params(("parallel",), 6 * _nbytes((tm, d), _F32) + (8 << 20)),
        name="rmsnorm",
    )(x, g.reshape(1, d))


def _rmsnorm_halo_body(x_ref, prev_ref, next_ref, g_ref, o_ref, *, tm, tr, tiles_per_seq):
    r = pl.program_id(1)
    row0 = pl.multiple_of(r * tr, tr)
    o_ref[pl.ds(row0, tr), :] = _rms(x_ref[...], g_ref[...]).astype(o_ref.dtype)

    @pl.when(r == 0)
    def _():
        pos = pl.program_id(0) % tiles_per_seq
        nxt = jnp.where(pos == tiles_per_seq - 1, 0.0, _rms(next_ref[...], g_ref[...]))
        prv = jnp.where(pos == 0, 0.0, _rms(prev_ref[...], g_ref[...]))
        o_ref[tm:tm + _HALO, :] = nxt.astype(o_ref.dtype)
        o_ref[tm + _HALO:tm + 2 * _HALO, :] = prv.astype(o_ref.dtype)


def _rmsnorm_halo(x, g, seq, tm):
    m, d = x.shape
    tr = _tile(tm, 256, _HALO)
    nt, nr = m // tm, tm // tr
    hb = tm // _HALO
    last = m // _HALO - 1
    return pl.pallas_call(
        functools.partial(_rmsnorm_halo_body, tm=tm, tr=tr, tiles_per_seq=seq // tm),
        out_shape=jax.ShapeDtypeStruct((nt, tm + 2 * _HALO, d), _BF16),
        grid=(nt, nr),
        in_specs=[
            pl.BlockSpec((tr, d), lambda i, r: (i * nr + r, 0)),
            pl.BlockSpec((_HALO, d), lambda i, r: (jnp.maximum(i * hb - 1, 0), 0)),
            pl.BlockSpec((_HALO, d), lambda i, r: (jnp.minimum((i + 1) * hb, last), 0)),
            pl.BlockSpec((1, d), lambda i, r: (0, 0)),
        ],
        out_specs=pl.BlockSpec((None, tm + 2 * _HALO, d), lambda i, r: (i, 0, 0)),
        compiler_params=_params(("parallel", "arbitrary"),
                                6 * _nbytes((tr, d), _F32) + 2 * _nbytes((tm + 2 * _HALO, d), _BF16) + (8 << 20)),
        name="rmsnorm_halo",
    )(x, x, x, g.reshape(1, d))


def _proj_cast_body(a_ref, b_ref, o_ref):
    o_ref[...] = _dot(a_ref[...], b_ref[...]).astype(o_ref.dtype)


def _proj_sigmoid_body(a_ref, b_ref, o_ref):
    o_ref[...] = jax.nn.sigmoid(_dot(a_ref[...], b_ref[...])).astype(o_ref.dtype)


def _proj_groupnorm_body(a_ref, b_ref, g_ref, o_ref, *, group):
    acc = _dot(a_ref[...], b_ref[...])
    for c in range(acc.shape[1] // group):
        cols = slice(c * group, (c + 1) * group)
        o_ref[:, cols] = _rms(acc[:, cols], g_ref[:, cols]).astype(o_ref.dtype)


def _proj_residual_body(a_ref, b_ref, r_ref, o_ref):
    o_ref[...] = r_ref[...] + _dot(a_ref[...], b_ref[...])


def _project(body, a, b, out_dtype, sections=None, extra=(), extra_specs=(), tile=1024, name="project"):
    m, k = a.shape
    sections = sections or ((0, b.shape[1]),)
    n = sum(width for _, width in sections)
    tm = _tile(m, tile, 16)
    tn = _tile(math.gcd(*(v for sec in sections for v in sec)), tile, _LANES)
    first_out, shifts, done = [], [], 0
    for start, width in sections:
        first_out.append(done // tn)
        shifts.append(start // tn - done // tn)
        done += width

    def src_block(j):
        shift = shifts[0]
        for first, s in zip(first_out[1:], shifts[1:]):
            shift = jnp.where(j >= first, s, shift)
        return j + shift

    vmem = (2 * _nbytes((tm, k), a.dtype) + 2 * _nbytes((k, tn), b.dtype)
            + 6 * _nbytes((tm, tn), _F32) + (4 << 20))
    return pl.pallas_call(
        body,
        out_shape=jax.ShapeDtypeStruct((m, n), out_dtype),
        grid=(m // tm, n // tn),
        in_specs=[pl.BlockSpec((tm, k), lambda i, j: (i, 0)),
                  pl.BlockSpec((k, tn), lambda i, j: (0, src_block(j))),
                  *[spec(tm, tn) for spec in extra_specs]],
        out_specs=pl.BlockSpec((tm, tn), lambda i, j: (i, j)),
        compiler_params=_params(("parallel", "arbitrary"), vmem),
        name=name,
    )(a, b, *extra)


def _row_vec_spec(tm, tn):
    return pl.BlockSpec((1, tn), lambda i, j: (0, j))


def _tile_spec(tm, tn):
    return pl.BlockSpec((tm, tn), lambda i, j: (i, j))


def _t5_bucket(rel, num_buckets):
    half = num_buckets // 2
    max_exact = half // 2
    ret = jnp.where(rel > 0, half, 0)
    n = jnp.abs(rel)
    nf = jnp.maximum(n, 1).astype(_F32)
    large = max_exact + (jnp.log(nf / max_exact) / math.log(_REL_MAX_DIST / max_exact)
                         * (half - max_exact)).astype(jnp.int32)
    large = jnp.minimum(large, half - 1)
    return ret + jnp.where(n < max_exact, n, large)


def _band_body(rb_ref, o_ref, *, tq, tk, m_lo, num_buckets):
    h = pl.program_id(0)
    m = pl.program_id(1) + m_lo
    row = lax.broadcasted_iota(jnp.int32, (tq, tk), 0)
    col = lax.broadcasted_iota(jnp.int32, (tq, tk), 1)
    bucket = _t5_bucket(m * tq + col - row, num_buckets)
    out = jnp.zeros((tq, tk), _F32)
    for b in range(num_buckets):
        out = jnp.where(bucket == b, rb_ref[h, b], out)
    o_ref[...] = out * _LOG2E


def _band_range(tq, tk):
    m_lo = -((_REL_MAX_DIST - 1 + tk + tq - 1) // tq)
    m_hi = (_REL_MAX_DIST - 1 + tq + tq - 1) // tq
    return m_lo, m_hi


def _bias_band(rel_bias, tq, tk):
    nbk, h = rel_bias.shape
    m_lo, m_hi = _band_range(tq, tk)
    nb = m_hi - m_lo + 1
    return pl.pallas_call(
        functools.partial(_band_body, tq=tq, tk=tk, m_lo=m_lo, num_buckets=nbk),
        out_shape=jax.ShapeDtypeStruct((h, nb, tq, tk), _F32),
        grid=(h, nb),
        in_specs=[pl.BlockSpec(memory_space=pltpu.SMEM)],
        out_specs=pl.BlockSpec((None, None, tq, tk), lambda i, j: (i, j, 0, 0)),
        compiler_params=_params(("parallel", "parallel"), 24 * _nbytes((tq, tk), _F32) + (4 << 20)),
        name="bias_band",
    )(rel_bias.T)


def _lane_fold(x, op):
    return functools.reduce(op, [x[:, c:c + _LANES] for c in range(0, x.shape[1], _LANES)])


def _attn_body(lam_ref, q_ref, k_ref, v_ref, band_ref, g_ref, o_ref, s_ref, p_ref,
               *, tq, tk, d, m_lo, m_hi, lambda_init):
    i = pl.program_id(2)
    nk = k_ref.shape[0] // tk
    ratio = tk // tq

    band_idx = [jnp.clip(j * ratio - i, m_lo, m_hi) - m_lo for j in range(nk)]

    def scores(mp):
        cols = slice(mp * d, (mp + 1) * d)
        qm = q_ref[:, cols]
        for j in range(nk):
            keys = slice(j * tk, (j + 1) * tk)
            s_ref[mp, :, keys] = lax.dot_general(qm, k_ref[keys, cols], _NT, preferred_element_type=_F32)

    def softmax_rows(mp):
        sums = []
        for r in range(0, tq, _ROW_GROUP):
            rows = slice(r, r + _ROW_GROUP)
            part_max = None
            for j in range(nk):
                keys = slice(j * tk, (j + 1) * tk)
                folded = _lane_fold(s_ref[mp, rows, keys] + band_ref[band_idx[j], rows, :], jnp.maximum)
                part_max = folded if part_max is None else jnp.maximum(part_max, folded)
            row_max = jnp.max(part_max, axis=-1, keepdims=True)
            part_sum = jnp.zeros((_ROW_GROUP, _LANES), _F32)
            for j in range(nk):
                keys = slice(j * tk, (j + 1) * tk)
                p = jnp.exp2(s_ref[mp, rows, keys] + (band_ref[band_idx[j], rows, :] - row_max))
                part_sum = part_sum + _lane_fold(p, jnp.add)
                p_ref[mp, rows, keys] = p.astype(p_ref.dtype)
            sums.append(jnp.sum(part_sum, axis=-1, keepdims=True))
        return jnp.concatenate(sums, axis=0)

    scores(0)
    scores(1)
    heads_out = []
    for mp in range(2):
        row_sum = softmax_rows(mp)
        heads_out.append(_dot(p_ref[mp], v_ref[...]) / row_sum)

    lv = lam_ref[...]
    lam = (jnp.exp(jnp.sum(lv[0:1] * lv[1:2], axis=-1, keepdims=True))
           - jnp.exp(jnp.sum(lv[2:3] * lv[3:4], axis=-1, keepdims=True)) + lambda_init)
    o = heads_out[0] - lam * heads_out[1]
    o_ref[...] = (_rms(o, g_ref[...]) * (1.0 - lambda_init)).astype(o_ref.dtype)


def _diff_attention(qk, v_src, band, lam_vecs, subln_g, batch, seq, heads, d, lambda_init, tq, tk):
    m = qk.shape[0]
    nq = seq // tq
    m_lo, m_hi = _band_range(tq, tk)
    nb = m_hi - m_lo + 1
    w = 2 * d
    vmem = (4 * _nbytes((seq, w), _BF16) + 2 * _nbytes((nb, tq, tk), _F32) + _nbytes((2, tq, seq), _F32)
            + _nbytes((2, tq, seq), _BF16) + 8 * _nbytes((tq, tk), _F32) + (6 << 20))
    return pl.pallas_call(
        functools.partial(_attn_body, tq=tq, tk=tk, d=d, m_lo=m_lo, m_hi=m_hi, lambda_init=lambda_init),
        out_shape=jax.ShapeDtypeStruct((m, heads * w), _BF16),
        grid=(batch, heads, nq),
        in_specs=[
            pl.BlockSpec((4, d), lambda b, h, i: (0, 0)),
            pl.BlockSpec((tq, w), lambda b, h, i: (b * nq + i, h)),
            pl.BlockSpec((seq, w), lambda b, h, i: (b, heads + h)),
            pl.BlockSpec((seq, w), lambda b, h, i: (b, h)),
            pl.BlockSpec((None, nb, tq, tk), lambda b, h, i: (h, 0, 0, 0)),
            pl.BlockSpec((1, w), lambda b, h, i: (0, 0)),
        ],
        out_specs=pl.BlockSpec((tq, w), lambda b, h, i: (b * nq + i, h)),
        scratch_shapes=[pltpu.VMEM((2, tq, seq), _F32), pltpu.VMEM((2, tq, seq), _BF16)],
        compiler_params=_params(("parallel", "parallel", "arbitrary"), vmem),
        name="diff_attention",
    )(lam_vecs, qk, qk, v_src, band, subln_g.reshape(1, w))


def _split(x):
    hi = x.astype(_BF16)
    return hi, (x - hi.astype(_F32)).astype(_BF16)


def _split_dot(lhs, parts, dims=None):
    if dims is None:
        return sum(_dot(lhs, part) for part in parts)
    return sum(lax.dot_general(part, lhs, dims, preferred_element_type=_F32) for part in parts)


def _gla_body(*refs, tt, dk, dv, hpb, reverse, final, q_scale):
    if final:
        (q_ref, k_ref, lr_ref, v_ref, wg_ref, bg_ref, fwd_ref, r_ref, gn_ref, o_ref, s_ref) = refs
    else:
        (q_ref, k_ref, lr_ref, v_ref, wg_ref, bg_ref, o_ref, s_ref) = refs
    c_len = _GLA_CHUNK
    nc = tt // c_len

    @pl.when(pl.program_id(2) == 0)
    def _():
        s_ref[...] = jnp.zeros_like(s_ref)

    row = lax.broadcasted_iota(jnp.int32, (tt, tt), 0)
    col = lax.broadcasted_iota(jnp.int32, (tt, tt), 1)
    causal = ((row // c_len) == (col // c_len)) & ((col >= row) if reverse else (col <= row))
    causal_ones = causal.astype(_BF16)
    ones_cols = jnp.ones((tt, _LANES), _BF16)
    stems = lr_ref[...].astype(_BF16)
    order = list(reversed(range(nc))) if reverse else list(range(nc))
    edge = 0 if reverse else c_len - 1

    def chunk_rows(x, c):
        return x[c * c_len:(c + 1) * c_len]

    def per_chunk(fn):
        return jnp.concatenate([fn(c) for c in range(nc)], axis=0)

    for hh in range(hpb):
        kc = slice(hh * dk, (hh + 1) * dk)
        vc = slice(hh * dv, (hh + 1) * dv)
        v = v_ref[:, vc]
        pre = _dot(stems, wg_ref[:, kc]) + bg_ref[:, kc]
        log_a = (jnp.minimum(pre, 0.0) - jnp.log(1.0 + jnp.exp(-jnp.abs(pre)))) * (_LOG2E / _GLA_TAU)
        log_a_parts = _split(log_a)
        b = _split_dot(causal_ones, log_a_parts)
        total = [chunk_rows(b, c)[edge:edge + 1] for c in range(nc)]
        start, run = {}, jnp.zeros((1, dk), _F32)
        for c in order:
            start[c] = run
            run = run + total[c]

        q_own = q_ref[:, kc] * q_scale * jnp.exp2(b)
        k_own = k_ref[:, kc] * jnp.exp2(per_chunk(lambda c: total[c] - chunk_rows(b, c)))
        q_dec = q_own.astype(_BF16)
        k_inv = (k_ref[:, kc] * jnp.exp2(-b)).astype(_BF16)
        att = jnp.where(causal, lax.dot_general(q_dec, k_inv, _NT, preferred_element_type=_F32), 0.0)

        def seen_keys(c):
            pos = order.index(c)
            return per_chunk(lambda c2: (chunk_rows(k_own, c2) * jnp.exp2(start[c] - start[c2] - total[c2])
                                         if order.index(c2) < pos else jnp.zeros((c_len, dk), _F32)))

        att = att + per_chunk(lambda c: (
            jnp.zeros((c_len, tt), _F32) if c == order[0] else
            lax.dot_general(chunk_rows(q_dec, c), seen_keys(c).astype(_BF16), _NT, preferred_element_type=_F32)))

        s = s_ref[hh]
        q_state = per_chunk(lambda c: chunk_rows(q_own, c) * jnp.exp2(start[c])).astype(_BF16)
        k_state = per_chunk(lambda c: chunk_rows(k_own, c) * jnp.exp2(run - start[c] - total[c])).astype(_BF16)
        o = _dot(att.astype(_BF16), v) + _dot(q_state, s.astype(_BF16))
        tile_decay = jnp.exp2(_split_dot(ones_cols, log_a_parts, _TN))[:, 0:1]
        s_ref[hh] = tile_decay * s + lax.dot_general(k_state, v, _TN, preferred_element_type=_F32)
        if final:
            o = _rms(o + fwd_ref[:, vc], gn_ref[...])
            r = r_ref[:, vc].astype(_F32)
            o_ref[:, vc] = (o * (r * jax.nn.sigmoid(r))).astype(o_ref.dtype)
        else:
            o_ref[:, vc] = o


def _gla(qk, stems, vgr, w_gate, b_gate, batch, seq, heads, dk, dv, v_col0, reverse,
         fwd=None, r_col0=None, norm_g=None):
    m = qk.shape[0]
    tt = _tile(seq, 256, _GLA_CHUNK)
    nt = seq // tt
    hpb = _tile(heads, 4, 1)
    ng = heads // hpb
    final = fwd is not None
    assert v_col0 % hpb == 0 and (r_col0 is None or r_col0 % hpb == 0)

    def rows(b, t):
        return b * nt + (nt - 1 - t if reverse else t)

    in_specs = [
        pl.BlockSpec((tt, hpb * dk), lambda b, g, t: (rows(b, t), g)),
        pl.BlockSpec((tt, hpb * dk), lambda b, g, t: (rows(b, t), ng + g)),
        pl.BlockSpec((tt, dk), lambda b, g, t: (rows(b, t), 0)),
        pl.BlockSpec((tt, hpb * dv), lambda b, g, t: (rows(b, t), v_col0 // hpb + g)),
        pl.BlockSpec((dk, hpb * dk), lambda b, g, t: (0, g)),
        pl.BlockSpec((1, hpb * dk), lambda b, g, t: (0, g)),
    ]
    args = [qk, qk, stems, vgr, w_gate, b_gate]
    if final:
        in_specs += [
            pl.BlockSpec((tt, hpb * dv), lambda b, g, t: (rows(b, t), g)),
            pl.BlockSpec((tt, hpb * dv), lambda b, g, t: (rows(b, t), r_col0 // hpb + g)),
            pl.BlockSpec((1, dv), lambda b, g, t: (0, 0)),
        ]
        args += [fwd, vgr, norm_g.reshape(1, dv)]
    return pl.pallas_call(
        functools.partial(_gla_body, tt=tt, dk=dk, dv=dv, hpb=hpb, reverse=reverse, final=final,
                          q_scale=dk ** -0.5),
        out_shape=jax.ShapeDtypeStruct((m, heads * dv), _BF16 if final else _F32),
        grid=(batch, ng, nt),
        in_specs=in_specs,
        out_specs=pl.BlockSpec((tt, hpb * dv), lambda b, g, t: (rows(b, t), g)),
        scratch_shapes=[pltpu.VMEM((hpb, dk, dv), _F32)],
        compiler_params=_params(("parallel", "parallel", "arbitrary"), 48 << 20),
        name="gla_bwd" if reverse else "gla_fwd",
    )(*args)


def _merge_body(oa_ref, ob_ref, pa_ref, pb_ref, ga_ref, gb_ref, o_ref):
    a = _dot(oa_ref[...], pa_ref[...])
    b = _dot(ob_ref[...], pb_ref[...])
    o_ref[...] = (ga_ref[...].astype(_F32) * a + gb_ref[...].astype(_F32) * b).astype(o_ref.dtype)


def _merge(o_a, o_b, p_a, p_b, gates):
    m, ka = o_a.shape
    kb = o_b.shape[1]
    n = p_a.shape[1]
    tm = _tile(m, 1024, 16)
    tn = _tile(n, 1024, _LANES)
    nj = n // tn
    vmem = (2 * _nbytes((tm, ka + kb), _BF16) + 2 * _nbytes((ka + kb, tn), _BF16)
            + 4 * _nbytes((tm, tn), _BF16) + 6 * _nbytes((tm, tn), _F32) + (4 << 20))
    return pl.pallas_call(
        _merge_body,
        out_shape=jax.ShapeDtypeStruct((m, n), _BF16),
        grid=(m // tm, nj),
        in_specs=[
            pl.BlockSpec((tm, ka), lambda i, j: (i, 0)),
            pl.BlockSpec((tm, kb), lambda i, j: (i, 0)),
            pl.BlockSpec((ka, tn), lambda i, j: (0, j)),
            pl.BlockSpec((kb, tn), lambda i, j: (0, j)),
            pl.BlockSpec((tm, tn), lambda i, j: (i, j)),
            pl.BlockSpec((tm, tn), lambda i, j: (i, nj + j)),
        ],
        out_specs=pl.BlockSpec((tm, tn), lambda i, j: (i, j)),
        compiler_params=_params(("parallel", "arbitrary"), vmem),
        name="branch_merge",
    )(o_a, o_b, p_a, p_b, gates, gates)


def _gelu_tanh(x):
    return x * (0.5 * (1.0 + jnp.tanh(math.sqrt(2.0 / math.pi) * (x + 0.044715 * (x * x * x)))))


def _ffn_up_weight_body(w_ref, o_ref, *, nblk):
    t = pl.program_id(0)
    q = 2 * (t // 4) + t % 2
    o_ref[...] = jnp.where(q < nblk, w_ref[...], 0.0).astype(o_ref.dtype)


def _ffn_up_weight(w_up, f):
    d = w_up.shape[0]
    blk = _FFN_TILE // 2
    assert f % blk == 0
    nblk = f // blk
    n_out = 4 * (-(-f // _FFN_TILE))

    def src(t):
        q = 2 * (t // 4) + t % 2
        return 0, jnp.minimum(jnp.where(t % 4 < 2, q, nblk + q), 2 * nblk - 1)

    return pl.pallas_call(
        functools.partial(_ffn_up_weight_body, nblk=nblk),
        out_shape=jax.ShapeDtypeStruct((d, n_out * blk), _BF16),
        grid=(n_out,),
        in_specs=[pl.BlockSpec((d, blk), src)],
        out_specs=pl.BlockSpec((d, blk), lambda t: (0, t)),
        compiler_params=_params(("parallel",), 6 * _nbytes((d, blk), _F32) + (4 << 20)),
        name="ffn_up_weight",
    )(w_up)


def _ffn_up_body(h_ref, wa_ref, wb_ref, cw_ref, cb_ref, o_ref, *, tm, last_width):
    def gated(width):
        cols = slice(0, width)
        a = _dot(h_ref[...], wa_ref[:, cols])
        g = _dot(h_ref[0:tm, :], wb_ref[:, cols])
        rows = a.shape[0]
        prev = pltpu.roll(a, 1, 0)[0:tm]
        nxt = pltpu.roll(a, rows - 1, 0)[0:tm]
        conv = prev * cw_ref[0:1, cols] + a[0:tm] * cw_ref[1:2, cols] + nxt * cw_ref[2:3, cols] + cb_ref[:, cols]
        o_ref[:, cols] = (_gelu_tanh(conv) * g).astype(o_ref.dtype)

    tf = o_ref.shape[1]
    if last_width == tf:
        gated(tf)
    else:
        is_last = pl.program_id(1) == pl.num_programs(1) - 1
        pl.when(jnp.logical_not(is_last))(functools.partial(gated, tf))
        pl.when(is_last)(functools.partial(gated, last_width))


def _ffn_up(h_tiles, w_ag, conv_w, conv_b, tm, f):
    nt, rows, d = h_tiles.shape
    tf = _FFN_TILE
    vmem = (2 * _nbytes((rows, d), _BF16) + 4 * _nbytes((d, tf), _BF16) + 2 * _nbytes((tm, tf), _BF16)
            + 8 * _nbytes((rows, tf), _F32) + (4 << 20))
    return pl.pallas_call(
        functools.partial(_ffn_up_body, tm=tm, last_width=f % tf or tf),
        out_shape=jax.ShapeDtypeStruct((nt * tm, f), _BF16),
        grid=(nt, w_ag.shape[1] // (2 * tf)),
        in_specs=[
            pl.BlockSpec((None, rows, d), lambda i, j: (i, 0, 0)),
            pl.BlockSpec((d, tf), lambda i, j: (0, 2 * j)),
            pl.BlockSpec((d, tf), lambda i, j: (0, 2 * j + 1)),
            pl.BlockSpec((3, tf), lambda i, j: (0, j)),
            pl.BlockSpec((1, tf), lambda i, j: (0, j)),
        ],
        out_specs=pl.BlockSpec((tm, tf), lambda i, j: (i, j)),
        compiler_params=_params(("parallel", "arbitrary"), vmem),
        name="ffn_up",
    )(h_tiles, w_ag, w_ag, conv_w, conv_b)


def _pad_cols(w, n):
    return jnp.pad(w, ((0, 0), (0, n - w.shape[1])))


def _prepare_weights(lambda_init, rel_bias, g_mix, w_in, q_norm_g, k_norm_g, lambda_q1, lambda_k1, lambda_q2,
                     lambda_k2, da_subln_g, w_gate_fwd, b_gate_fwd, w_gate_bwd, b_gate_bwd, gla_norm_g,
                     w_branch_a, w_branch_b, w_out, g_ffn, w_up, conv_w, conv_b, w_down):
    heads = rel_bias.shape[1]
    d = q_norm_g.shape[-1]
    da = heads * 2 * d
    rank, gla_k = w_gate_fwd.shape
    dv = gla_norm_g.shape[-1]
    gla_v = w_branch_b.shape[0]
    gh = gla_v // dv
    dk = gla_k // gh
    f = conv_b.shape[-1]
    d_model = w_in.shape[0]
    widths = [da, da, da, gla_k, gla_k, gla_v, gla_v, rank, rank, d_model, d_model]
    starts = [0]
    for wd in widths:
        starts.append(starts[-1] + wd)
    sec = lambda s: (starts[s], widths[s])

    w_stems = _pad_cols(w_in[:, starts[7]:starts[9]].astype(_BF16), dk)
    w_gates = w_in[:, starts[9]:].astype(_BF16)
    qk_gain = jnp.concatenate([jnp.tile(q_norm_g * (d ** -0.5 * _LOG2E), 2 * heads), jnp.tile(k_norm_g, 2 * heads)])

    def gate_weight(w, first_row):
        return jnp.zeros((dk, gla_k), _BF16).at[first_row:first_row + rank].set(w.astype(_BF16))

    fp = -(-f // _FFN_TILE) * _FFN_TILE
    return dict(
        heads=heads, d=d, gh=gh, dk=dk, dv=dv, lambda_init=lambda_init,
        rel_bias=rel_bias, g_mix=g_mix, g_ffn=g_ffn,
        w_in=w_in.astype(_BF16), sec_qk=(sec(0), sec(1)), sec_vgr=(sec(2), sec(5), sec(6)),
        sec_gla_qk=(sec(3), sec(4)),
        w_stems=w_stems, w_gates=w_gates, qk_gain=qk_gain.reshape(1, -1),
        lam_vecs=jnp.stack([lambda_q1, lambda_k1, lambda_q2, lambda_k2]),
        da_subln_g=da_subln_g, gla_norm_g=gla_norm_g,
        wg_fwd=gate_weight(w_gate_fwd, 0), wg_bwd=gate_weight(w_gate_bwd, rank),
        bg_fwd=b_gate_fwd.reshape(1, -1), bg_bwd=b_gate_bwd.reshape(1, -1),
        p_a=w_branch_a.astype(_BF16), p_b=w_branch_b.astype(_BF16), w_out=w_out.astype(_BF16),
        f=f, w_up=_ffn_up_weight(w_up, f),
        conv_w=_pad_cols(conv_w, fp), conv_b=_pad_cols(conv_b.reshape(1, f), fp),
        w_down=w_down.astype(_BF16),
    )


def _encoder_layer(x3, p, band, tq, tk):
    batch, seq, d_model = x3.shape
    x = x3.reshape(batch * seq, d_model)
    heads, d, gh, dk, dv = p["heads"], p["d"], p["gh"], p["dk"], p["dv"]

    h = _rmsnorm(x, p["g_mix"])
    qk = _project(functools.partial(_proj_groupnorm_body, group=d), h, p["w_in"], _BF16, p["sec_qk"],
                  extra=(p["qk_gain"],), extra_specs=(_row_vec_spec,), name="proj_qk")
    vgr = _project(_proj_cast_body, h, p["w_in"], _BF16, p["sec_vgr"], name="proj_vgr")
    gla_qk = _project(_proj_cast_body, h, p["w_in"], _F32, p["sec_gla_qk"], name="proj_gla_qk")
    stems = _project(_proj_cast_body, h, p["w_stems"], _F32, name="proj_stems")
    gates = _project(_proj_sigmoid_body, h, p["w_gates"], _BF16, name="proj_gates")

    o_a = _diff_attention(qk, vgr, band, p["lam_vecs"], p["da_subln_g"], batch, seq, heads, d,
                          p["lambda_init"], tq, tk)

    v_col0 = (heads * 2 * d) // dv
    r_col0 = v_col0 + gh
    fwd = _gla(gla_qk, stems, vgr, p["wg_fwd"], p["bg_fwd"], batch, seq, gh, dk, dv, v_col0, reverse=False)
    o_b = _gla(gla_qk, stems, vgr, p["wg_bwd"], p["bg_bwd"], batch, seq, gh, dk, dv, v_col0, reverse=True,
               fwd=fwd, r_col0=r_col0, norm_g=p["gla_norm_g"])

    merged = _merge(o_a, o_b, p["p_a"], p["p_b"], gates)
    x1 = _project(_proj_residual_body, merged, p["w_out"], _F32, extra=(x,), extra_specs=(_tile_spec,),
                  name="proj_out")

    tm = _tile(seq, 1024, 2 * _HALO)
    h2 = _rmsnorm_halo(x1, p["g_ffn"], seq, tm)
    act = _ffn_up(h2, p["w_up"], p["conv_w"], p["conv_b"], tm, p["f"])
    y = _project(_proj_residual_body, act, p["w_down"], _F32, extra=(x1,), extra_specs=(_tile_spec,),
                 tile=512, name="ffn_down")
    return y.reshape(batch, seq, d_model)


def kernel(x_prompt, x_sample, rel_bias, g_mix, w_in, q_norm_g, k_norm_g, lambda_q1, lambda_k1, lambda_q2,
           lambda_k2, da_subln_g, w_gate_fwd, b_gate_fwd, w_gate_bwd, b_gate_bwd, gla_norm_g, w_branch_a,
           w_branch_b, w_out, g_ffn, w_up, conv_w, conv_b, w_down):
    layer_weights = (g_mix, w_in, q_norm_g, k_norm_g, lambda_q1, lambda_k1, lambda_q2, lambda_k2, da_subln_g,
                     w_gate_fwd, b_gate_fwd, w_gate_bwd, b_gate_bwd, gla_norm_g, w_branch_a, w_branch_b, w_out,
                     g_ffn, w_up, conv_w, conv_b, w_down)
    min_seq = min(x_prompt.shape[1], x_sample.shape[1])
    tq = _tile(min_seq, 512, _LANES)
    tk = _tile(min_seq, 512, tq)
    band = _bias_band(rel_bias, tq, tk)
    y_prompt, y_sample = x_prompt, x_sample
    for l in range(g_mix.shape[0]):
        lambda_init = 0.8 - 0.6 * math.exp(-0.3 * l)
        p = _prepare_weights(lambda_init, rel_bias, *(w[l] for w in layer_weights))
        y_prompt = _encoder_layer(y_prompt, p, band, tq, tk)
        y_sample = _encoder_layer(y_sample, p, band, tq, tk)
    return (y_prompt, y_sample)
```

```python
import functools
import math

import jax
import jax.numpy as jnp
from jax import lax
from jax.experimental import pallas as pl
from jax.experimental.pallas import tpu as pltpu

_EPS = 1e-6
_GLA_CHUNK = 64
_GLA_TAU = 16.0
_REL_MAX_DIST = 128
_LOG2E = math.log2(math.e)
_LANES = 128
_HALO = 16
_ROW_GROUP = 16
_FFN_TILE = 512
_V7X_VMEM_BYTES = 64 * 1024 * 1024
_VMEM_BUDGET = _V7X_VMEM_BYTES - 8 * 1024 * 1024

_F32 = jnp.float32
_BF16 = jnp.bfloat16
_NT = (((1,), (1,)), ((), ()))
_TN = (((0,), (0,)), ((), ()))


def _params(semantics, vmem_bytes):
    return pltpu.CompilerParams(dimension_semantics=semantics,
                                vmem_limit_bytes=int(min(vmem_bytes, _VMEM_BUDGET)))


def _tile(n, pref, align):
    t = min(pref, n)
    t -= t % align
    while t > align and n % t:
        t -= align
    assert t >= align and n % t == 0, (n, pref, align)
    return t


def _nbytes(shape, dtype):
    return math.prod(shape) * jnp.dtype(dtype).itemsize


def _dot(a, b):
    return jnp.dot(a, b, preferred_element_type=_F32)


def _rms(x, g):
    ms = jnp.mean(x * x, axis=-1, keepdims=True)
    return x * lax.rsqrt(ms + _EPS) * g


def _rmsnorm_body(x_ref, g_ref, o_ref):
    o_ref[...] = _rms(x_ref[...], g_ref[...]).astype(o_ref.dtype)


def _rmsnorm(x, g):
    m, d = x.shape
    tm = _tile(m, 256, 8)
    return pl.pallas_call(
        _rmsnorm_body,
        out_shape=jax.ShapeDtypeStruct((m, d), _BF16),
        grid=(m // tm,),
        in_specs=[pl.BlockSpec((tm, d), lambda i: (i, 0)), pl.BlockSpec((1, d), lambda i: (0, 0))],
        out_specs=pl.BlockSpec((tm, d), lambda i: (i, 0)),
        compiler_params=_params(("parallel",), 6 * _nbytes((tm, d), _F32) + (8 << 20)),
        name="rmsnorm",
    )(x, g.reshape(1, d))


def _rmsnorm_halo_body(x_ref, prev_ref, next_ref, g_ref, o_ref, *, tm, tr, tiles_per_seq):
    r = pl.program_id(1)
    row0 = pl.multiple_of(r * tr, tr)
    o_ref[pl.ds(row0, tr), :] = _rms(x_ref[...], g_ref[...]).astype(o_ref.dtype)

    @pl.when(r == 0)
    def _():
        pos = pl.program_id(0) % tiles_per_seq
        nxt = jnp.where(pos == tiles_per_seq - 1, 0.0, _rms(next_ref[...], g_ref[...]))
        prv = jnp.where(pos == 0, 0.0, _rms(prev_ref[...], g_ref[...]))
        o_ref[tm:tm + _HALO, :] = nxt.astype(o_ref.dtype)
        o_ref[tm + _HALO:tm + 2 * _HALO, :] = prv.astype(o_ref.dtype)


def _rmsnorm_halo(x, g, seq, tm):
    m, d = x.shape
    tr = _tile(tm, 256, _HALO)
    nt, nr = m // tm, tm // tr
    hb = tm // _HALO
    last = m // _HALO - 1
    return pl.pallas_call(
        functools.partial(_rmsnorm_halo_body, tm=tm, tr=tr, tiles_per_seq=seq // tm),
        out_shape=jax.ShapeDtypeStruct((nt, tm + 2 * _HALO, d), _BF16),
        grid=(nt, nr),
        in_specs=[
            pl.BlockSpec((tr, d), lambda i, r: (i * nr + r, 0)),
            pl.BlockSpec((_HALO, d), lambda i, r: (jnp.maximum(i * hb - 1, 0), 0)),
            pl.BlockSpec((_HALO, d), lambda i, r: (jnp.minimum((i + 1) * hb, last), 0)),
            pl.BlockSpec((1, d), lambda i, r: (0, 0)),
        ],
        out_specs=pl.BlockSpec((None, tm + 2 * _HALO, d), lambda i, r: (i, 0, 0)),
        compiler_params=_params(("parallel", "arbitrary"),
                                6 * _nbytes((tr, d), _F32) + 2 * _nbytes((tm + 2 * _HALO, d), _BF16) + (8 << 20)),
        name="rmsnorm_halo",
    )(x, x, x, g.reshape(1, d))


def _cast_body(w_ref, o_ref):
    o_ref[...] = w_ref[...].astype(o_ref.dtype)


def _cast_bf16(w):
    r, c = w.shape
    tr = _tile(r, 64, 16)
    return pl.pallas_call(
        _cast_body,
        out_shape=jax.ShapeDtypeStruct((r, c), _BF16),
        grid=(r // tr,),
        in_specs=[pl.BlockSpec((tr, c), lambda i: (i, 0))],
        out_specs=pl.BlockSpec((tr, c), lambda i: (i, 0)),
        compiler_params=_params(("parallel",), 4 * _nbytes((tr, c), _F32) + (4 << 20)),
        name="cast_bf16",
    )(w)


def _proj_cast_body(a_ref, b_ref, o_ref):
    o_ref[...] = _dot(a_ref[...], b_ref[...]).astype(o_ref.dtype)


def _proj_sigmoid_body(a_ref, b_ref, o_ref):
    o_ref[...] = (0.5 * jnp.tanh(0.5 * _dot(a_ref[...], b_ref[...])) + 0.5).astype(o_ref.dtype)


def _proj_groupnorm_body(a_ref, b_ref, g_ref, o_ref, *, group):
    acc = _dot(a_ref[...], b_ref[...])
    for c in range(acc.shape[1] // group):
        cols = slice(c * group, (c + 1) * group)
        o_ref[:, cols] = _rms(acc[:, cols], g_ref[:, cols]).astype(o_ref.dtype)


def _proj_residual_body(a_ref, b_ref, r_ref, o_ref):
    o_ref[...] = r_ref[...] + _dot(a_ref[...], b_ref[...])


def _project(body, a, b, out_dtype, sections=None, extra=(), extra_specs=(), tile=1024, name="project"):
    m, k = a.shape
    sections = sections or ((0, b.shape[1]),)
    n = sum(width for _, width in sections)
    tm = _tile(m, tile, 16)
    tn = _tile(math.gcd(*(v for sec in sections for v in sec)), tile, _LANES)
    first_out, shifts, done = [], [], 0
    for start, width in sections:
        first_out.append(done // tn)
        shifts.append(start // tn - done // tn)
        done += width

    def src_block(j):
        shift = shifts[0]
        for first, s in zip(first_out[1:], shifts[1:]):
            shift = jnp.where(j >= first, s, shift)
        return j + shift

    vmem = (2 * _nbytes((tm, k), a.dtype) + 2 * _nbytes((k, tn), b.dtype)
            + 6 * _nbytes((tm, tn), _F32) + (4 << 20))
    return pl.pallas_call(
        body,
        out_shape=jax.ShapeDtypeStruct((m, n), out_dtype),
        grid=(m // tm, n // tn),
        in_specs=[pl.BlockSpec((tm, k), lambda i, j: (i, 0)),
                  pl.BlockSpec((k, tn), lambda i, j: (0, src_block(j))),
                  *[spec(tm, tn) for spec in extra_specs]],
        out_specs=pl.BlockSpec((tm, tn), lambda i, j: (i, j)),
        compiler_params=_params(("parallel", "arbitrary"), vmem),
        name=name,
    )(a, b, *extra)


def _row_vec_spec(tm, tn):
    return pl.BlockSpec((1, tn), lambda i, j: (0, j))


def _tile_spec(tm, tn):
    return pl.BlockSpec((tm, tn), lambda i, j: (i, j))


def _t5_bucket(rel, num_buckets):
    half = num_buckets // 2
    max_exact = half // 2
    ret = jnp.where(rel > 0, half, 0)
    n = jnp.abs(rel)
    nf = jnp.maximum(n, 1).astype(_F32)
    large = max_exact + (jnp.log(nf / max_exact) / math.log(_REL_MAX_DIST / max_exact)
                         * (half - max_exact)).astype(jnp.int32)
    large = jnp.minimum(large, half - 1)
    return ret + jnp.where(n < max_exact, n, large)


def _band_body(rb_ref, o_ref, *, tq, tk, m_lo, num_buckets):
    h = pl.program_id(0)
    mi = pl.program_id(1)
    last = pl.num_programs(1) - 1

    @pl.when(mi == 0)
    def _():
        o_ref[...] = jnp.full((tq, tk), rb_ref[h, num_buckets // 2 - 1] * _LOG2E, _F32)

    @pl.when(mi == last)
    def _():
        o_ref[...] = jnp.full((tq, tk), rb_ref[h, num_buckets - 1] * _LOG2E, _F32)

    @pl.when((mi > 0) & (mi < last))
    def _():
        row = lax.broadcasted_iota(jnp.int32, (tq, tk), 0)
        col = lax.broadcasted_iota(jnp.int32, (tq, tk), 1)
        bucket = _t5_bucket((mi + m_lo) * tq + col - row, num_buckets)
        out = jnp.zeros((tq, tk), _F32)
        for b in range(num_buckets):
            out = jnp.where(bucket == b, rb_ref[h, b], out)
        o_ref[...] = out * _LOG2E


def _band_range(tq, tk):
    m_lo = -((_REL_MAX_DIST - 1 + tk + tq - 1) // tq)
    m_hi = (_REL_MAX_DIST - 1 + tq + tq - 1) // tq
    return m_lo, m_hi


def _bias_band(rel_bias, tq, tk):
    nbk, h = rel_bias.shape
    m_lo, m_hi = _band_range(tq, tk)
    nb = m_hi - m_lo + 1
    return pl.pallas_call(
        functools.partial(_band_body, tq=tq, tk=tk, m_lo=m_lo, num_buckets=nbk),
        out_shape=jax.ShapeDtypeStruct((h, nb, tq, tk), _F32),
        grid=(h, nb),
        in_specs=[pl.BlockSpec(memory_space=pltpu.SMEM)],
        out_specs=pl.BlockSpec((None, None, tq, tk), lambda i, j: (i, j, 0, 0)),
        compiler_params=_params(("parallel", "parallel"), 24 * _nbytes((tq, tk), _F32) + (4 << 20)),
        name="bias_band",
    )(rel_bias.T)


def _lane_fold(x, op):
    return functools.reduce(op, [x[:, c:c + _LANES] for c in range(0, x.shape[1], _LANES)])


def _attn_body(lam_ref, q_ref, k_ref, v_ref, band_ref, g_ref, o_ref, s_ref, p_ref,
               *, tq, tk, d, m_lo, m_hi, lambda_init):
    i = pl.program_id(2)
    nk = k_ref.shape[0] // tk
    ratio = tk // tq

    band_idx = [jnp.clip(j * ratio - i, m_lo, m_hi) - m_lo for j in range(nk)]

    def scores(mp):
        cols = slice(mp * d, (mp + 1) * d)
        qm = q_ref[:, cols]
        for j in range(nk):
            keys = slice(j * tk, (j + 1) * tk)
            s_ref[mp, :, keys] = lax.dot_general(qm, k_ref[keys, cols], _NT, preferred_element_type=_F32)

    def softmax_rows(mp):
        sums = []
        for r in range(0, tq, _ROW_GROUP):
            rows = slice(r, r + _ROW_GROUP)
            part_max = None
            for j in range(nk):
                keys = slice(j * tk, (j + 1) * tk)
                folded = _lane_fold(s_ref[mp, rows, keys] + band_ref[band_idx[j], rows, :], jnp.maximum)
                part_max = folded if part_max is None else jnp.maximum(part_max, folded)
            row_max = jnp.max(part_max, axis=-1, keepdims=True)
            part_sum = jnp.zeros((_ROW_GROUP, _LANES), _F32)
            for j in range(nk):
                keys = slice(j * tk, (j + 1) * tk)
                p = jnp.exp2(s_ref[mp, rows, keys] + (band_ref[band_idx[j], rows, :] - row_max))
                part_sum = part_sum + _lane_fold(p, jnp.add)
                p_ref[mp, rows, keys] = p.astype(p_ref.dtype)
            sums.append(jnp.sum(part_sum, axis=-1, keepdims=True))
        return jnp.concatenate(sums, axis=0)

    scores(0)
    scores(1)
    heads_out = []
    for mp in range(2):
        row_sum = softmax_rows(mp)
        heads_out.append(_dot(p_ref[mp], v_ref[...]) / row_sum)

    lv = lam_ref[...]
    lam = (jnp.exp(jnp.sum(lv[0:1] * lv[1:2], axis=-1, keepdims=True))
           - jnp.exp(jnp.sum(lv[2:3] * lv[3:4], axis=-1, keepdims=True)) + lambda_init)
    o = heads_out[0] - lam * heads_out[1]
    o_ref[...] = (_rms(o, g_ref[...]) * (1.0 - lambda_init)).astype(o_ref.dtype)


def _diff_attention(qk, v_src, band, lam_vecs, subln_g, batch, seq, heads, d, lambda_init, tq, tk):
    m = qk.shape[0]
    nq = seq // tq
    m_lo, m_hi = _band_range(tq, tk)
    nb = m_hi - m_lo + 1
    w = 2 * d
    vmem = (4 * _nbytes((seq, w), _BF16) + 2 * _nbytes((nb, tq, tk), _F32) + _nbytes((2, tq, seq), _F32)
            + _nbytes((2, tq, seq), _BF16) + 8 * _nbytes((tq, tk), _F32) + (6 << 20))
    return pl.pallas_call(
        functools.partial(_attn_body, tq=tq, tk=tk, d=d, m_lo=m_lo, m_hi=m_hi, lambda_init=lambda_init),
        out_shape=jax.ShapeDtypeStruct((m, heads * w), _BF16),
        grid=(batch, heads, nq),
        in_specs=[
            pl.BlockSpec((4, d), lambda b, h, i: (0, 0)),
            pl.BlockSpec((tq, w), lambda b, h, i: (b * nq + i, h)),
            pl.BlockSpec((seq, w), lambda b, h, i: (b, heads + h)),
            pl.BlockSpec((seq, w), lambda b, h, i: (b, h)),
            pl.BlockSpec((None, nb, tq, tk), lambda b, h, i: (h, 0, 0, 0)),
            pl.BlockSpec((1, w), lambda b, h, i: (0, 0)),
        ],
        out_specs=pl.BlockSpec((tq, w), lambda b, h, i: (b * nq + i, h)),
        scratch_shapes=[pltpu.VMEM((2, tq, seq), _F32), pltpu.VMEM((2, tq, seq), _BF16)],
        compiler_params=_params(("parallel", "parallel", "arbitrary"), vmem),
        name="diff_attention",
    )(lam_vecs, qk, qk, v_src, band, subln_g.reshape(1, w))


def _split(x):
    hi = x.astype(_BF16)
    return hi, (x - hi.astype(_F32)).astype(_BF16)


def _split_dot(lhs, parts, dims=None):
    if dims is None:
        return sum(_dot(lhs, part) for part in parts)
    return sum(lax.dot_general(part, lhs, dims, preferred_element_type=_F32) for part in parts)


def _gla_body(*refs, tt, dk, dv, hpb, reverse, final, q_scale):
    if final:
        (q_ref, k_ref, lr_ref, v_ref, wg_ref, bg_ref, fwd_ref, r_ref, gn_ref, o_ref, s_ref) = refs
    else:
        (q_ref, k_ref, lr_ref, v_ref, wg_ref, bg_ref, o_ref, s_ref) = refs
    c_len = _GLA_CHUNK
    nc = tt // c_len

    @pl.when(pl.program_id(2) == 0)
    def _():
        s_ref[...] = jnp.zeros_like(s_ref)

    row = lax.broadcasted_iota(jnp.int32, (tt, tt), 0)
    col = lax.broadcasted_iota(jnp.int32, (tt, tt), 1)
    causal = ((row // c_len) == (col // c_len)) & ((col >= row) if reverse else (col <= row))
    causal_ones = causal.astype(_BF16)
    ones_cols = jnp.ones((tt, _LANES), _BF16)
    stems = lr_ref[...].astype(_BF16)
    order = list(reversed(range(nc))) if reverse else list(range(nc))
    edge = 0 if reverse else c_len - 1

    def chunk_rows(x, c):
        return x[c * c_len:(c + 1) * c_len]

    def per_chunk(fn):
        return jnp.concatenate([fn(c) for c in range(nc)], axis=0)

    for hh in range(hpb):
        kc = slice(hh * dk, (hh + 1) * dk)
        vc = slice(hh * dv, (hh + 1) * dv)
        v = v_ref[:, vc]
        pre = _dot(stems, wg_ref[:, kc]) + bg_ref[:, kc]
        log_a = (jnp.minimum(pre, 0.0) - jnp.log(1.0 + jnp.exp(-jnp.abs(pre)))) * (_LOG2E / _GLA_TAU)
        log_a_parts = _split(log_a)
        b = _split_dot(causal_ones, log_a_parts)
        total = [chunk_rows(b, c)[edge:edge + 1] for c in range(nc)]
        start, run = {}, jnp.zeros((1, dk), _F32)
        for c in order:
            start[c] = run
            run = run + total[c]

        q_own = q_ref[:, kc] * q_scale * jnp.exp2(b)
        k_own = k_ref[:, kc] * jnp.exp2(per_chunk(lambda c: total[c] - chunk_rows(b, c)))
        q_dec = q_own.astype(_BF16)
        k_inv = (k_ref[:, kc] * jnp.exp2(-b)).astype(_BF16)
        att = jnp.where(causal, lax.dot_general(q_dec, k_inv, _NT, preferred_element_type=_F32), 0.0)

        def seen_keys(c):
            pos = order.index(c)
            return per_chunk(lambda c2: (chunk_rows(k_own, c2) * jnp.exp2(start[c] - start[c2] - total[c2])
                                         if order.index(c2) < pos else jnp.zeros((c_len, dk), _F32)))

        att = att + per_chunk(lambda c: (
            jnp.zeros((c_len, tt), _F32) if c == order[0] else
            lax.dot_general(chunk_rows(q_dec, c), seen_keys(c).astype(_BF16), _NT, preferred_element_type=_F32)))

        s = s_ref[hh]
        q_state = per_chunk(lambda c: chunk_rows(q_own, c) * jnp.exp2(start[c])).astype(_BF16)
        k_state = per_chunk(lambda c: chunk_rows(k_own, c) * jnp.exp2(run - start[c] - total[c])).astype(_BF16)
        o = _dot(att.astype(_BF16), v) + _dot(q_state, s.astype(_BF16))
        tile_decay = jnp.exp2(_split_dot(ones_cols, log_a_parts, _TN))[:, 0:1]
        s_ref[hh] = tile_decay * s + lax.dot_general(k_state, v, _TN, preferred_element_type=_F32)
        if final:
            o = _rms(o + fwd_ref[:, vc], gn_ref[...])
            r = r_ref[:, vc].astype(_F32)
            o_ref[:, vc] = (o * (r * (0.5 * jnp.tanh(0.5 * r) + 0.5))).astype(o_ref.dtype)
        else:
            o_ref[:, vc] = o


def _gla(qk, stems, vgr, w_gate, b_gate, batch, seq, heads, dk, dv, v_col0, reverse,
         fwd=None, r_col0=None, norm_g=None):
    m = qk.shape[0]
    tt = _tile(seq, 256, _GLA_CHUNK)
    nt = seq // tt
    hpb = _tile(heads, 4, 1)
    ng = heads // hpb
    final = fwd is not None
    assert v_col0 % hpb == 0 and (r_col0 is None or r_col0 % hpb == 0)

    def rows(b, t):
        return b * nt + (nt - 1 - t if reverse else t)

    in_specs = [
        pl.BlockSpec((tt, hpb * dk), lambda b, g, t: (rows(b, t), g)),
        pl.BlockSpec((tt, hpb * dk), lambda b, g, t: (rows(b, t), ng + g)),
        pl.BlockSpec((tt, dk), lambda b, g, t: (rows(b, t), 0)),
        pl.BlockSpec((tt, hpb * dv), lambda b, g, t: (rows(b, t), v_col0 // hpb + g)),
        pl.BlockSpec((dk, hpb * dk), lambda b, g, t: (0, g)),
        pl.BlockSpec((1, hpb * dk), lambda b, g, t: (0, g)),
    ]
    args = [qk, qk, stems, vgr, w_gate, b_gate]
    if final:
        in_specs += [
            pl.BlockSpec((tt, hpb * dv), lambda b, g, t: (rows(b, t), g)),
            pl.BlockSpec((tt, hpb * dv), lambda b, g, t: (rows(b, t), r_col0 // hpb + g)),
            pl.BlockSpec((1, dv), lambda b, g, t: (0, 0)),
        ]
        args += [fwd, vgr, norm_g.reshape(1, dv)]
    return pl.pallas_call(
        functools.partial(_gla_body, tt=tt, dk=dk, dv=dv, hpb=hpb, reverse=reverse, final=final,
                          q_scale=dk ** -0.5),
        out_shape=jax.ShapeDtypeStruct((m, heads * dv), _BF16 if final else _F32),
        grid=(batch, ng, nt),
        in_specs=in_specs,
        out_specs=pl.BlockSpec((tt, hpb * dv), lambda b, g, t: (rows(b, t), g)),
        scratch_shapes=[pltpu.VMEM((hpb, dk, dv), _F32)],
        compiler_params=_params(("parallel", "parallel", "arbitrary"), 48 << 20),
        name="gla_bwd" if reverse else "gla_fwd",
    )(*args)


def _merge_body(oa_ref, ob_ref, pa_ref, pb_ref, ga_ref, gb_ref, o_ref):
    a = _dot(oa_ref[...], pa_ref[...])
    b = _dot(ob_ref[...], pb_ref[...])
    o_ref[...] = (ga_ref[...].astype(_F32) * a + gb_ref[...].astype(_F32) * b).astype(o_ref.dtype)


def _merge(o_a, o_b, p_a, p_b, gates):
    m, ka = o_a.shape
    kb = o_b.shape[1]
    n = p_a.shape[1]
    tm = _tile(m, 1024, 16)
    tn = _tile(n, 1024, _LANES)
    nj = n // tn
    vmem = (2 * _nbytes((tm, ka + kb), _BF16) + 2 * _nbytes((ka + kb, tn), _BF16)
            + 4 * _nbytes((tm, tn), _BF16) + 6 * _nbytes((tm, tn), _F32) + (4 << 20))
    return pl.pallas_call(
        _merge_body,
        out_shape=jax.ShapeDtypeStruct((m, n), _BF16),
        grid=(m // tm, nj),
        in_specs=[
            pl.BlockSpec((tm, ka), lambda i, j: (i, 0)),
            pl.BlockSpec((tm, kb), lambda i, j: (i, 0)),
            pl.BlockSpec((ka, tn), lambda i, j: (0, j)),
            pl.BlockSpec((kb, tn), lambda i, j: (0, j)),
            pl.BlockSpec((tm, tn), lambda i, j: (i, j)),
            pl.BlockSpec((tm, tn), lambda i, j: (i, nj + j)),
        ],
        out_specs=pl.BlockSpec((tm, tn), lambda i, j: (i, j)),
        compiler_params=_params(("parallel", "arbitrary"), vmem),
        name="branch_merge",
    )(o_a, o_b, p_a, p_b, gates, gates)


def _gelu_tanh(x):
    return x * (0.5 * (1.0 + jnp.tanh(math.sqrt(2.0 / math.pi) * (x + 0.044715 * (x * x * x)))))


def _ffn_up_weight_body(w_ref, o_ref, *, nblk):
    t = pl.program_id(0)
    q = 2 * (t // 4) + t % 2
    o_ref[...] = jnp.where(q < nblk, w_ref[...], 0.0).astype(o_ref.dtype)


def _ffn_up_weight(w_up, f):
    d = w_up.shape[0]
    blk = _FFN_TILE // 2
    assert f % blk == 0
    nblk = f // blk
    n_out = 4 * (-(-f // _FFN_TILE))

    def src(t):
        q = 2 * (t // 4) + t % 2
        return 0, jnp.minimum(jnp.where(t % 4 < 2, q, nblk + q), 2 * nblk - 1)

    return pl.pallas_call(
        functools.partial(_ffn_up_weight_body, nblk=nblk),
        out_shape=jax.ShapeDtypeStruct((d, n_out * blk), _BF16),
        grid=(n_out,),
        in_specs=[pl.BlockSpec((d, blk), src)],
        out_specs=pl.BlockSpec((d, blk), lambda t: (0, t)),
        compiler_params=_params(("parallel",), 6 * _nbytes((d, blk), _F32) + (4 << 20)),
        name="ffn_up_weight",
    )(w_up)


def _ffn_up_body(h_ref, wa_ref, wb_ref, cw_ref, cb_ref, o_ref, *, tm, last_width):
    def gated(width):
        cols = slice(0, width)
        a = _dot(h_ref[...], wa_ref[:, cols])
        g = _dot(h_ref[0:tm, :], wb_ref[:, cols])
        rows = a.shape[0]
        prev = pltpu.roll(a, 1, 0)[0:tm]
        nxt = pltpu.roll(a, rows - 1, 0)[0:tm]
        conv = prev * cw_ref[0:1, cols] + a[0:tm] * cw_ref[1:2, cols] + nxt * cw_ref[2:3, cols] + cb_ref[:, cols]
        o_ref[:, cols] = (_gelu_tanh(conv) * g).astype(o_ref.dtype)

    tf = o_ref.shape[1]
    if last_width == tf:
        gated(tf)
    else:
        is_last = pl.program_id(1) == pl.num_programs(1) - 1
        pl.when(jnp.logical_not(is_last))(functools.partial(gated, tf))
        pl.when(is_last)(functools.partial(gated, last_width))


def _ffn_up(h_tiles, w_ag, conv_w, conv_b, tm, f):
    nt, rows, d = h_tiles.shape
    tf = _FFN_TILE
    vmem = (2 * _nbytes((rows, d), _BF16) + 4 * _nbytes((d, tf), _BF16) + 2 * _nbytes((tm, tf), _BF16)
            + 8 * _nbytes((rows, tf), _F32) + (4 << 20))
    return pl.pallas_call(
        functools.partial(_ffn_up_body, tm=tm, last_width=f % tf or tf),
        out_shape=jax.ShapeDtypeStruct((nt * tm, f), _BF16),
        grid=(nt, w_ag.shape[1] // (2 * tf)),
        in_specs=[
            pl.BlockSpec((None, rows, d), lambda i, j: (i, 0, 0)),
            pl.BlockSpec((d, tf), lambda i, j: (0, 2 * j)),
            pl.BlockSpec((d, tf), lambda i, j: (0, 2 * j + 1)),
            pl.BlockSpec((3, tf), lambda i, j: (0, j)),
            pl.BlockSpec((1, tf), lambda i, j: (0, j)),
        ],
        out_specs=pl.BlockSpec((tm, tf), lambda i, j: (i, j)),
        compiler_params=_params(("parallel", "arbitrary"), vmem),
        name="ffn_up",
    )(h_tiles, w_ag, w_ag, conv_w, conv_b)


def _pad_cols(w, n):
    return jnp.pad(w, ((0, 0), (0, n - w.shape[1])))


def _prepare_weights(lambda_init, rel_bias, g_mix, w_in, q_norm_g, k_norm_g, lambda_q1, lambda_k1, lambda_q2,
                     lambda_k2, da_subln_g, w_gate_fwd, b_gate_fwd, w_gate_bwd, b_gate_bwd, gla_norm_g,
                     w_branch_a, w_branch_b, w_out, g_ffn, w_up, conv_w, conv_b, w_down):
    heads = rel_bias.shape[1]
    d = q_norm_g.shape[-1]
    da = heads * 2 * d
    rank, gla_k = w_gate_fwd.shape
    dv = gla_norm_g.shape[-1]
    gla_v = w_branch_b.shape[0]
    gh = gla_v // dv
    dk = gla_k // gh
    f = conv_b.shape[-1]
    d_model = w_in.shape[0]
    widths = [da, da, da, gla_k, gla_k, gla_v, gla_v, rank, rank, d_model, d_model]
    starts = [0]
    for wd in widths:
        starts.append(starts[-1] + wd)
    sec = lambda s: (starts[s], widths[s])

    w_stems = _pad_cols(w_in[:, starts[7]:starts[9]].astype(_BF16), dk)
    w_gates = w_in[:, starts[9]:].astype(_BF16)
    qk_gain = jnp.concatenate([jnp.tile(q_norm_g * (d ** -0.5 * _LOG2E), 2 * heads), jnp.tile(k_norm_g, 2 * heads)])

    def gate_weight(w, first_row):
        return jnp.zeros((dk, gla_k), _BF16).at[first_row:first_row + rank].set(w.astype(_BF16))

    fp = -(-f // _FFN_TILE) * _FFN_TILE
    return dict(
        heads=heads, d=d, gh=gh, dk=dk, dv=dv, lambda_init=lambda_init,
        rel_bias=rel_bias, g_mix=g_mix, g_ffn=g_ffn,
        w_in=_cast_bf16(w_in), sec_qk=(sec(0), sec(1)), sec_vgr=(sec(2), sec(5), sec(6)),
        sec_gla_qk=(sec(3), sec(4)),
        w_stems=w_stems, w_gates=w_gates, qk_gain=qk_gain.reshape(1, -1),
        lam_vecs=jnp.stack([lambda_q1, lambda_k1, lambda_q2, lambda_k2]),
        da_subln_g=da_subln_g, gla_norm_g=gla_norm_g,
        wg_fwd=gate_weight(w_gate_fwd, 0), wg_bwd=gate_weight(w_gate_bwd, rank),
        bg_fwd=b_gate_fwd.reshape(1, -1), bg_bwd=b_gate_bwd.reshape(1, -1),
        p_a=w_branch_a.astype(_BF16), p_b=w_branch_b.astype(_BF16), w_out=w_out.astype(_BF16),
        f=f, w_up=_ffn_up_weight(w_up, f),
        conv_w=_pad_cols(conv_w, fp), conv_b=_pad_cols(conv_b.reshape(1, f), fp),
        w_down=w_down.astype(_BF16),
    )


def _encoder_layer(x3, p, band, tq, tk):
    batch, seq, d_model = x3.shape
    x = x3.reshape(batch * seq, d_model)
    heads, d, gh, dk, dv = p["heads"], p["d"], p["gh"], p["dk"], p["dv"]

    h = _rmsnorm(x, p["g_mix"])
    qk = _project(functools.partial(_proj_groupnorm_body, group=d), h, p["w_in"], _BF16, p["sec_qk"],
                  extra=(p["qk_gain"],), extra_specs=(_row_vec_spec,), name="proj_qk")
    vgr = _project(_proj_cast_body, h, p["w_in"], _BF16, p["sec_vgr"], name="proj_vgr")
    gla_qk = _project(_proj_cast_body, h, p["w_in"], _F32, p["sec_gla_qk"], name="proj_gla_qk")
    stems = _project(_proj_cast_body, h, p["w_stems"], _F32, name="proj_stems")
    gates = _project(_proj_sigmoid_body, h, p["w_gates"], _BF16, name="proj_gates")

    o_a = _diff_attention(qk, vgr, band, p["lam_vecs"], p["da_subln_g"], batch, seq, heads, d,
                          p["lambda_init"], tq, tk)

    v_col0 = (heads * 2 * d) // dv
    r_col0 = v_col0 + gh
    fwd = _gla(gla_qk, stems, vgr, p["wg_fwd"], p["bg_fwd"], batch, seq, gh, dk, dv, v_col0, reverse=False)
    o_b = _gla(gla_qk, stems, vgr, p["wg_bwd"], p["bg_bwd"], batch, seq, gh, dk, dv, v_col0, reverse=True,
               fwd=fwd, r_col0=r_col0, norm_g=p["gla_norm_g"])

    merged = _merge(o_a, o_b, p["p_a"], p["p_b"], gates)
    x1 = _project(_proj_residual_body, merged, p["w_out"], _F32, extra=(x,), extra_specs=(_tile_spec,),
                  name="proj_out")

    tm = _tile(seq, 1024, 2 * _HALO)
    h2 = _rmsnorm_halo(x1, p["g_ffn"], seq, tm)
    act = _ffn_up(h2, p["w_up"], p["conv_w"], p["conv_b"], tm, p["f"])
    y = _project(_proj_residual_body, act, p["w_down"], _F32, extra=(x1,), extra_specs=(_tile_spec,),
                 tile=512, name="ffn_down")
    return y.reshape(batch, seq, d_model)


def kernel(x_prompt, x_sample, rel_bias, g_mix, w_in, q_norm_g, k_norm_g, lambda_q1, lambda_k1, lambda_q2,
           lambda_k2, da_subln_g, w_gate_fwd, b_gate_fwd, w_gate_bwd, b_gate_bwd, gla_norm_g, w_branch_a,
           w_branch_b, w_out, g_ffn, w_up, conv_w, conv_b, w_down):
    layer_weights = (g_mix, w_in, q_norm_g, k_norm_g, lambda_q1, lambda_k1, lambda_q2, lambda_k2, da_subln_g,
                     w_gate_fwd, b_gate_fwd, w_gate_bwd, b_gate_bwd, gla_norm_g, w_branch_a, w_branch_b, w_out,
                     g_ffn, w_up, conv_w, conv_b, w_down)
    min_seq = min(x_prompt.shape[1], x_sample.shape[1])
    tq = _tile(min_seq, 512, _LANES)
    tk = _tile(min_seq, 512, tq)
    band = _bias_band(rel_bias, tq, tk)
    y_prompt, y_sample = x_prompt, x_sample
    for l in range(g_mix.shape[0]):
        lambda_init = 0.8 - 0.6 * math.exp(-0.3 * l)
        p = _prepare_weights(lambda_init, rel_bias, *(w[l] for w in layer_weights))
        y_prompt = _encoder_layer(y_prompt, p, band, tq, tk)
        y_sample = _encoder_layer(y_sample, p, band, tq, tk)
    return (y_prompt, y_sample)
```

```python
import functools
import math

import jax
import jax.numpy as jnp
from jax import lax
from jax.experimental import pallas as pl
from jax.experimental.pallas import tpu as pltpu

_EPS = 1e-6
_GLA_CHUNK = 64
_GLA_TAU = 16.0
_REL_MAX_DIST = 128
_LOG2E = math.log2(math.e)
_LANES = 128
_HALO = 16
_ROW_GROUP = 16
_FFN_TILE = 512
_DOWN_TILE = 512
_V7X_VMEM_BYTES = 64 * 1024 * 1024
_VMEM_BUDGET = _V7X_VMEM_BYTES - 8 * 1024 * 1024

_F32 = jnp.float32
_BF16 = jnp.bfloat16
_NT = (((1,), (1,)), ((), ()))
_TN = (((0,), (0,)), ((), ()))


def _params(semantics, vmem_bytes):
    return pltpu.CompilerParams(dimension_semantics=semantics,
                                vmem_limit_bytes=int(min(vmem_bytes, _VMEM_BUDGET)))


def _tile(n, pref, align):
    t = min(pref, n)
    t -= t % align
    while t > align and n % t:
        t -= align
    assert t >= align and n % t == 0, (n, pref, align)
    return t


def _nbytes(shape, dtype):
    return math.prod(shape) * jnp.dtype(dtype).itemsize


def _dot(a, b):
    return jnp.dot(a, b, preferred_element_type=_F32)


def _rms(x, g):
    ms = jnp.mean(x * x, axis=-1, keepdims=True)
    return x * lax.rsqrt(ms + _EPS) * g


def _rmsnorm_body(x_ref, g_ref, o_ref):
    o_ref[...] = _rms(x_ref[...], g_ref[...]).astype(o_ref.dtype)


def _rmsnorm(x, g):
    m, d = x.shape
    tm = _tile(m, 256, 8)
    return pl.pallas_call(
        _rmsnorm_body,
        out_shape=jax.ShapeDtypeStruct((m, d), _BF16),
        grid=(m // tm,),
        in_specs=[pl.BlockSpec((tm, d), lambda i: (i, 0)), pl.BlockSpec((1, d), lambda i: (0, 0))],
        out_specs=pl.BlockSpec((tm, d), lambda i: (i, 0)),
        compiler_params=_params(("parallel",), 6 * _nbytes((tm, d), _F32) + (8 << 20)),
        name="rmsnorm",
    )(x, g.reshape(1, d))


def _rmsnorm_halo_body(x_ref, prev_ref, next_ref, g_ref, o_ref, *, tm, tr, tiles_per_seq):
    r = pl.program_id(1)
    row0 = pl.multiple_of(r * tr, tr)
    o_ref[pl.ds(row0, tr), :] = _rms(x_ref[...], g_ref[...]).astype(o_ref.dtype)

    @pl.when(r == 0)
    def _():
        pos = pl.program_id(0) % tiles_per_seq
        nxt = jnp.where(pos == tiles_per_seq - 1, 0.0, _rms(next_ref[...], g_ref[...]))
        prv = jnp.where(pos == 0, 0.0, _rms(prev_ref[...], g_ref[...]))
        o_ref[tm:tm + _HALO, :] = nxt.astype(o_ref.dtype)
        o_ref[tm + _HALO:tm + 2 * _HALO, :] = prv.astype(o_ref.dtype)


def _rmsnorm_halo(x, g, seq, tm):
    m, d = x.shape
    tr = _tile(tm, 256, _HALO)
    nt, nr = m // tm, tm // tr
    hb = tm // _HALO
    last = m // _HALO - 1
    return pl.pallas_call(
        functools.partial(_rmsnorm_halo_body, tm=tm, tr=tr, tiles_per_seq=seq // tm),
        out_shape=jax.ShapeDtypeStruct((nt, tm + 2 * _HALO, d), _BF16),
        grid=(nt, nr),
        in_specs=[
            pl.BlockSpec((tr, d), lambda i, r: (i * nr + r, 0)),
            pl.BlockSpec((_HALO, d), lambda i, r: (jnp.maximum(i * hb - 1, 0), 0)),
            pl.BlockSpec((_HALO, d), lambda i, r: (jnp.minimum((i + 1) * hb, last), 0)),
            pl.BlockSpec((1, d), lambda i, r: (0, 0)),
        ],
        out_specs=pl.BlockSpec((None, tm + 2 * _HALO, d), lambda i, r: (i, 0, 0)),
        compiler_params=_params(("parallel", "arbitrary"),
                                6 * _nbytes((tr, d), _F32) + 2 * _nbytes((tm + 2 * _HALO, d), _BF16) + (8 << 20)),
        name="rmsnorm_halo",
    )(x, x, x, g.reshape(1, d))


def _proj_cast_body(a_ref, b_ref, o_ref):
    o_ref[...] = _dot(a_ref[...], b_ref[...]).astype(o_ref.dtype)


def _proj_sigmoid_body(a_ref, b_ref, o_ref):
    o_ref[...] = (0.5 * jnp.tanh(0.5 * _dot(a_ref[...], b_ref[...])) + 0.5).astype(o_ref.dtype)


def _proj_groupnorm_body(a_ref, b_ref, g_ref, o_ref, *, group):
    acc = _dot(a_ref[...], b_ref[...])
    for c in range(acc.shape[1] // group):
        cols = slice(c * group, (c + 1) * group)
        o_ref[:, cols] = _rms(acc[:, cols], g_ref[:, cols]).astype(o_ref.dtype)


def _proj_residual_body(a_ref, b_ref, r_ref, o_ref):
    o_ref[...] = r_ref[...] + _dot(a_ref[...], b_ref[...])


def _project(body, a, b, out_dtype, sections=None, extra=(), extra_specs=(), tile=1024, name="project"):
    m, k = a.shape
    if b.ndim == 3:
        assert sections is None and b.shape[2] == tile
        sections = ((0, b.shape[0] * tile),)
    sections = sections or ((0, b.shape[1]),)
    n = sum(width for _, width in sections)
    tm = _tile(m, tile, 16)
    tn = _tile(math.gcd(*(v for sec in sections for v in sec)), tile, _LANES)
    first_out, shifts, done = [], [], 0
    for start, width in sections:
        first_out.append(done // tn)
        shifts.append(start // tn - done // tn)
        done += width

    def src_block(j):
        shift = shifts[0]
        for first, s in zip(first_out[1:], shifts[1:]):
            shift = jnp.where(j >= first, s, shift)
        return j + shift

    vmem = (2 * _nbytes((tm, k), a.dtype) + 2 * _nbytes((k, tn), b.dtype)
            + 6 * _nbytes((tm, tn), _F32) + (4 << 20))
    return pl.pallas_call(
        body,
        out_shape=jax.ShapeDtypeStruct((m, n), out_dtype),
        grid=(m // tm, n // tn),
        in_specs=[pl.BlockSpec((tm, k), lambda i, j: (i, 0)),
                  (pl.BlockSpec((None, k, tn), lambda i, j: (j, 0, 0)) if b.ndim == 3 else
                   pl.BlockSpec((k, tn), lambda i, j: (0, src_block(j)))),
                  *[spec(tm, tn) for spec in extra_specs]],
        out_specs=pl.BlockSpec((tm, tn), lambda i, j: (i, j)),
        compiler_params=_params(("parallel", "arbitrary"), vmem),
        name=name,
    )(a, b, *extra)


def _row_vec_spec(tm, tn):
    return pl.BlockSpec((1, tn), lambda i, j: (0, j))


def _tile_spec(tm, tn):
    return pl.BlockSpec((tm, tn), lambda i, j: (i, j))


def _t5_bucket(rel, num_buckets):
    half = num_buckets // 2
    max_exact = half // 2
    ret = jnp.where(rel > 0, half, 0)
    n = jnp.abs(rel)
    nf = jnp.maximum(n, 1).astype(_F32)
    large = max_exact + (jnp.log(nf / max_exact) / math.log(_REL_MAX_DIST / max_exact)
                         * (half - max_exact)).astype(jnp.int32)
    large = jnp.minimum(large, half - 1)
    return ret + jnp.where(n < max_exact, n, large)


def _band_body(rb_ref, o_ref, *, tq, tk, m_lo, num_buckets):
    h = pl.program_id(0)
    mi = pl.program_id(1)
    last = pl.num_programs(1) - 1

    @pl.when(mi == 0)
    def _():
        o_ref[...] = jnp.full((tq, tk), rb_ref[h, num_buckets // 2 - 1] * _LOG2E, _F32)

    @pl.when(mi == last)
    def _():
        o_ref[...] = jnp.full((tq, tk), rb_ref[h, num_buckets - 1] * _LOG2E, _F32)

    @pl.when((mi > 0) & (mi < last))
    def _():
        row = lax.broadcasted_iota(jnp.int32, (tq, tk), 0)
        col = lax.broadcasted_iota(jnp.int32, (tq, tk), 1)
        bucket = _t5_bucket((mi + m_lo) * tq + col - row, num_buckets)
        out = jnp.zeros((tq, tk), _F32)
        for b in range(num_buckets):
            out = jnp.where(bucket == b, rb_ref[h, b], out)
        o_ref[...] = out * _LOG2E


def _band_range(tq, tk):
    m_lo = -((_REL_MAX_DIST - 1 + tk + tq - 1) // tq)
    m_hi = (_REL_MAX_DIST - 1 + tq + tq - 1) // tq
    return m_lo, m_hi


def _bias_band(rel_bias, tq, tk):
    nbk, h = rel_bias.shape
    m_lo, m_hi = _band_range(tq, tk)
    nb = m_hi - m_lo + 1
    return pl.pallas_call(
        functools.partial(_band_body, tq=tq, tk=tk, m_lo=m_lo, num_buckets=nbk),
        out_shape=jax.ShapeDtypeStruct((h, nb, tq, tk), _F32),
        grid=(h, nb),
        in_specs=[pl.BlockSpec(memory_space=pltpu.SMEM)],
        out_specs=pl.BlockSpec((None, None, tq, tk), lambda i, j: (i, j, 0, 0)),
        compiler_params=_params(("parallel", "parallel"), 24 * _nbytes((tq, tk), _F32) + (4 << 20)),
        name="bias_band",
    )(rel_bias.T)


def _lane_fold(x, op):
    return functools.reduce(op, [x[:, c:c + _LANES] for c in range(0, x.shape[1], _LANES)])


def _attn_body(lam_ref, q_ref, k_ref, v_ref, band_ref, g_ref, o_ref, s_ref, p_ref,
               *, tq, tk, d, m_lo, m_hi, lambda_init):
    i = pl.program_id(2)
    nk = k_ref.shape[0] // tk
    ratio = tk // tq

    band_idx = [jnp.clip(j * ratio - i, m_lo, m_hi) - m_lo for j in range(nk)]

    def scores(mp):
        cols = slice(mp * d, (mp + 1) * d)
        qm = q_ref[:, cols]
        for j in range(nk):
            keys = slice(j * tk, (j + 1) * tk)
            s_ref[mp, :, keys] = lax.dot_general(qm, k_ref[keys, cols], _NT, preferred_element_type=_F32)

    def softmax_rows(mp):
        sums = []
        for r in range(0, tq, _ROW_GROUP):
            rows = slice(r, r + _ROW_GROUP)
            part_max = None
            for j in range(nk):
                keys = slice(j * tk, (j + 1) * tk)
                folded = _lane_fold(s_ref[mp, rows, keys] + band_ref[band_idx[j], rows, :], jnp.maximum)
                part_max = folded if part_max is None else jnp.maximum(part_max, folded)
            row_max = jnp.max(part_max, axis=-1, keepdims=True)
            part_sum = jnp.zeros((_ROW_GROUP, _LANES), _F32)
            for j in range(nk):
                keys = slice(j * tk, (j + 1) * tk)
                p = jnp.exp2(s_ref[mp, rows, keys] + (band_ref[band_idx[j], rows, :] - row_max))
                part_sum = part_sum + _lane_fold(p, jnp.add)
                p_ref[mp, rows, keys] = p.astype(p_ref.dtype)
            sums.append(jnp.sum(part_sum, axis=-1, keepdims=True))
        return jnp.concatenate(sums, axis=0)

    scores(0)
    scores(1)
    heads_out = []
    for mp in range(2):
        row_sum = softmax_rows(mp)
        heads_out.append(_dot(p_ref[mp], v_ref[...]) / row_sum)

    lv = lam_ref[...]
    lam = (jnp.exp(jnp.sum(lv[0:1] * lv[1:2], axis=-1, keepdims=True))
           - jnp.exp(jnp.sum(lv[2:3] * lv[3:4], axis=-1, keepdims=True)) + lambda_init)
    o = heads_out[0] - lam * heads_out[1]
    o_ref[...] = (_rms(o, g_ref[...]) * (1.0 - lambda_init)).astype(o_ref.dtype)


def _diff_attention(qk, v_src, band, lam_vecs, subln_g, batch, seq, heads, d, lambda_init, tq, tk):
    m = qk.shape[0]
    nq = seq // tq
    m_lo, m_hi = _band_range(tq, tk)
    nb = m_hi - m_lo + 1
    w = 2 * d
    vmem = (4 * _nbytes((seq, w), _BF16) + 2 * _nbytes((nb, tq, tk), _F32) + _nbytes((2, tq, seq), _F32)
            + _nbytes((2, tq, seq), _BF16) + 8 * _nbytes((tq, tk), _F32) + (6 << 20))
    return pl.pallas_call(
        functools.partial(_attn_body, tq=tq, tk=tk, d=d, m_lo=m_lo, m_hi=m_hi, lambda_init=lambda_init),
        out_shape=jax.ShapeDtypeStruct((m, heads * w), _BF16),
        grid=(batch, heads, nq),
        in_specs=[
            pl.BlockSpec((4, d), lambda b, h, i: (0, 0)),
            pl.BlockSpec((tq, w), lambda b, h, i: (b * nq + i, h)),
            pl.BlockSpec((seq, w), lambda b, h, i: (b, heads + h)),
            pl.BlockSpec((seq, w), lambda b, h, i: (b, h)),
            pl.BlockSpec((None, nb, tq, tk), lambda b, h, i: (h, 0, 0, 0)),
            pl.BlockSpec((1, w), lambda b, h, i: (0, 0)),
        ],
        out_specs=pl.BlockSpec((tq, w), lambda b, h, i: (b * nq + i, h)),
        scratch_shapes=[pltpu.VMEM((2, tq, seq), _F32), pltpu.VMEM((2, tq, seq), _BF16)],
        compiler_params=_params(("parallel", "parallel", "arbitrary"), vmem),
        name="diff_attention",
    )(lam_vecs, qk, qk, v_src, band, subln_g.reshape(1, w))


def _split(x):
    hi = x.astype(_BF16)
    return hi, (x - hi.astype(_F32)).astype(_BF16)


def _split_dot(lhs, parts, dims=None):
    if dims is None:
        return sum(_dot(lhs, part) for part in parts)
    return sum(lax.dot_general(part, lhs, dims, preferred_element_type=_F32) for part in parts)


def _gla_body(*refs, tt, dk, dv, hpb, reverse, final, q_scale):
    if final:
        (q_ref, k_ref, lr_ref, v_ref, wg_ref, bg_ref, fwd_ref, r_ref, gn_ref, o_ref, s_ref) = refs
    else:
        (q_ref, k_ref, lr_ref, v_ref, wg_ref, bg_ref, o_ref, s_ref) = refs
    c_len = _GLA_CHUNK
    nc = tt // c_len

    @pl.when(pl.program_id(2) == 0)
    def _():
        s_ref[...] = jnp.zeros_like(s_ref)

    row = lax.broadcasted_iota(jnp.int32, (tt, tt), 0)
    col = lax.broadcasted_iota(jnp.int32, (tt, tt), 1)
    causal = ((row // c_len) == (col // c_len)) & ((col >= row) if reverse else (col <= row))
    causal_ones = causal.astype(_BF16)
    ones_cols = jnp.ones((tt, _LANES), _BF16)
    stems = lr_ref[...].astype(_BF16)
    order = list(reversed(range(nc))) if reverse else list(range(nc))
    edge = 0 if reverse else c_len - 1

    def chunk_rows(x, c):
        return x[c * c_len:(c + 1) * c_len]

    def per_chunk(fn):
        return jnp.concatenate([fn(c) for c in range(nc)], axis=0)

    for hh in range(hpb):
        kc = slice(hh * dk, (hh + 1) * dk)
        vc = slice(hh * dv, (hh + 1) * dv)
        v = v_ref[:, vc]
        pre = _dot(stems, wg_ref[:, kc]) + bg_ref[:, kc]
        log_a = (jnp.minimum(pre, 0.0) - jnp.log(1.0 + jnp.exp(-jnp.abs(pre)))) * (_LOG2E / _GLA_TAU)
        log_a_parts = _split(log_a)
        b = _split_dot(causal_ones, log_a_parts)
        total = [chunk_rows(b, c)[edge:edge + 1] for c in range(nc)]
        start, run = {}, jnp.zeros((1, dk), _F32)
        for c in order:
            start[c] = run
            run = run + total[c]

        q_own = q_ref[:, kc] * q_scale * jnp.exp2(b)
        k_own = k_ref[:, kc] * jnp.exp2(per_chunk(lambda c: total[c] - chunk_rows(b, c)))
        q_dec = q_own.astype(_BF16)
        k_inv = (k_ref[:, kc] * jnp.exp2(-b)).astype(_BF16)
        att = jnp.where(causal, lax.dot_general(q_dec, k_inv, _NT, preferred_element_type=_F32), 0.0)

        def seen_keys(c):
            pos = order.index(c)
            return per_chunk(lambda c2: (chunk_rows(k_own, c2) * jnp.exp2(start[c] - start[c2] - total[c2])
                                         if order.index(c2) < pos else jnp.zeros((c_len, dk), _F32)))

        att = att + per_chunk(lambda c: (
            jnp.zeros((c_len, tt), _F32) if c == order[0] else
            lax.dot_general(chunk_rows(q_dec, c), seen_keys(c).astype(_BF16), _NT, preferred_element_type=_F32)))

        s = s_ref[hh]
        q_state = per_chunk(lambda c: chunk_rows(q_own, c) * jnp.exp2(start[c])).astype(_BF16)
        k_state = per_chunk(lambda c: chunk_rows(k_own, c) * jnp.exp2(run - start[c] - total[c])).astype(_BF16)
        o = _dot(att.astype(_BF16), v) + _dot(q_state, s.astype(_BF16))
        tile_decay = jnp.exp2(_split_dot(ones_cols, log_a_parts, _TN))[:, 0:1]
        s_ref[hh] = tile_decay * s + lax.dot_general(k_state, v, _TN, preferred_element_type=_F32)
        if final:
            o = _rms(o + fwd_ref[:, vc], gn_ref[...])
            r = r_ref[:, vc].astype(_F32)
            o_ref[:, vc] = (o * (r * (0.5 * jnp.tanh(0.5 * r) + 0.5))).astype(o_ref.dtype)
        else:
            o_ref[:, vc] = o


def _gla(qk, stems, vgr, w_gate, b_gate, batch, seq, heads, dk, dv, v_col0, reverse,
         fwd=None, r_col0=None, norm_g=None):
    m = qk.shape[0]
    tt = _tile(seq, 256, _GLA_CHUNK)
    nt = seq // tt
    hpb = _tile(heads, 4, 1)
    ng = heads // hpb
    final = fwd is not None
    assert v_col0 % hpb == 0 and (r_col0 is None or r_col0 % hpb == 0)

    def rows(b, t):
        return b * nt + (nt - 1 - t if reverse else t)

    in_specs = [
        pl.BlockSpec((tt, hpb * dk), lambda b, g, t: (rows(b, t), g)),
        pl.BlockSpec((tt, hpb * dk), lambda b, g, t: (rows(b, t), ng + g)),
        pl.BlockSpec((tt, dk), lambda b, g, t: (rows(b, t), 0)),
        pl.BlockSpec((tt, hpb * dv), lambda b, g, t: (rows(b, t), v_col0 // hpb + g)),
        pl.BlockSpec((dk, hpb * dk), lambda b, g, t: (0, g)),
        pl.BlockSpec((1, hpb * dk), lambda b, g, t: (0, g)),
    ]
    args = [qk, qk, stems, vgr, w_gate, b_gate]
    if final:
        in_specs += [
            pl.BlockSpec((tt, hpb * dv), lambda b, g, t: (rows(b, t), g)),
            pl.BlockSpec((tt, hpb * dv), lambda b, g, t: (rows(b, t), r_col0 // hpb + g)),
            pl.BlockSpec((1, dv), lambda b, g, t: (0, 0)),
        ]
        args += [fwd, vgr, norm_g.reshape(1, dv)]
    return pl.pallas_call(
        functools.partial(_gla_body, tt=tt, dk=dk, dv=dv, hpb=hpb, reverse=reverse, final=final,
                          q_scale=dk ** -0.5),
        out_shape=jax.ShapeDtypeStruct((m, heads * dv), _BF16 if final else _F32),
        grid=(batch, ng, nt),
        in_specs=in_specs,
        out_specs=pl.BlockSpec((tt, hpb * dv), lambda b, g, t: (rows(b, t), g)),
        scratch_shapes=[pltpu.VMEM((hpb, dk, dv), _F32)],
        compiler_params=_params(("parallel", "parallel", "arbitrary"), 48 << 20),
        name="gla_bwd" if reverse else "gla_fwd",
    )(*args)


def _merge_body(oa_ref, ob_ref, pa_ref, pb_ref, ga_ref, gb_ref, o_ref):
    a = _dot(oa_ref[...], pa_ref[...])
    b = _dot(ob_ref[...], pb_ref[...])
    o_ref[...] = (ga_ref[...].astype(_F32) * a + gb_ref[...].astype(_F32) * b).astype(o_ref.dtype)


def _merge(o_a, o_b, p_a, p_b, gates):
    m, ka = o_a.shape
    kb = o_b.shape[1]
    n = p_a.shape[1]
    tm = _tile(m, 1024, 16)
    tn = _tile(n, 1024, _LANES)
    nj = n // tn
    vmem = (2 * _nbytes((tm, ka + kb), _BF16) + 2 * _nbytes((ka + kb, tn), _BF16)
            + 4 * _nbytes((tm, tn), _BF16) + 6 * _nbytes((tm, tn), _F32) + (4 << 20))
    return pl.pallas_call(
        _merge_body,
        out_shape=jax.ShapeDtypeStruct((m, n), _BF16),
        grid=(m // tm, nj),
        in_specs=[
            pl.BlockSpec((tm, ka), lambda i, j: (i, 0)),
            pl.BlockSpec((tm, kb), lambda i, j: (i, 0)),
            pl.BlockSpec((ka, tn), lambda i, j: (0, j)),
            pl.BlockSpec((kb, tn), lambda i, j: (0, j)),
            pl.BlockSpec((tm, tn), lambda i, j: (i, j)),
            pl.BlockSpec((tm, tn), lambda i, j: (i, nj + j)),
        ],
        out_specs=pl.BlockSpec((tm, tn), lambda i, j: (i, j)),
        compiler_params=_params(("parallel", "arbitrary"), vmem),
        name="branch_merge",
    )(o_a, o_b, p_a, p_b, gates, gates)


def _gelu_tanh(x):
    return x * (0.5 * (1.0 + jnp.tanh(math.sqrt(2.0 / math.pi) * (x + 0.044715 * (x * x * x)))))


def _ffn_up_weight_body(w_ref, o_ref, *, nblk):
    t = pl.program_id(0)
    q = 2 * (t // 4) + t % 2
    o_ref[...] = jnp.where(q < nblk, w_ref[...], 0.0).astype(o_ref.dtype)


def _ffn_up_weight(w_up, f):
    d = w_up.shape[0]
    blk = _FFN_TILE // 2
    assert f % blk == 0
    nblk = f // blk
    n_out = 4 * (-(-f // _FFN_TILE))

    def src(t):
        q = 2 * (t // 4) + t % 2
        return 0, jnp.minimum(jnp.where(t % 4 < 2, q, nblk + q), 2 * nblk - 1)

    return pl.pallas_call(
        functools.partial(_ffn_up_weight_body, nblk=nblk),
        out_shape=jax.ShapeDtypeStruct((d, n_out * blk), _BF16),
        grid=(n_out,),
        in_specs=[pl.BlockSpec((d, blk), src)],
        out_specs=pl.BlockSpec((d, blk), lambda t: (0, t)),
        compiler_params=_params(("parallel",), 6 * _nbytes((d, blk), _F32) + (4 << 20)),
        name="ffn_up_weight",
    )(w_up)


def _ffn_up_body(h_ref, wa_ref, wb_ref, cw_ref, cb_ref, o_ref, *, tm, last_width):
    def gated(width):
        cols = slice(0, width)
        a = _dot(h_ref[...], wa_ref[:, cols])
        g = _dot(h_ref[0:tm, :], wb_ref[:, cols])
        rows = a.shape[0]
        prev = pltpu.roll(a, 1, 0)[0:tm]
        nxt = pltpu.roll(a, rows - 1, 0)[0:tm]
        conv = prev * cw_ref[0:1, cols] + a[0:tm] * cw_ref[1:2, cols] + nxt * cw_ref[2:3, cols] + cb_ref[:, cols]
        o_ref[:, cols] = (_gelu_tanh(conv) * g).astype(o_ref.dtype)

    tf = o_ref.shape[1]
    if last_width == tf:
        gated(tf)
    else:
        is_last = pl.program_id(1) == pl.num_programs(1) - 1
        pl.when(jnp.logical_not(is_last))(functools.partial(gated, tf))
        pl.when(is_last)(functools.partial(gated, last_width))


def _ffn_up(h_tiles, w_ag, conv_w, conv_b, tm, f):
    nt, rows, d = h_tiles.shape
    tf = _FFN_TILE
    vmem = (2 * _nbytes((rows, d), _BF16) + 4 * _nbytes((d, tf), _BF16) + 2 * _nbytes((tm, tf), _BF16)
            + 8 * _nbytes((rows, tf), _F32) + (4 << 20))
    return pl.pallas_call(
        functools.partial(_ffn_up_body, tm=tm, last_width=f % tf or tf),
        out_shape=jax.ShapeDtypeStruct((nt * tm, f), _BF16),
        grid=(nt, w_ag.shape[1] // (2 * tf)),
        in_specs=[
            pl.BlockSpec((None, rows, d), lambda i, j: (i, 0, 0)),
            pl.BlockSpec((d, tf), lambda i, j: (0, 2 * j)),
            pl.BlockSpec((d, tf), lambda i, j: (0, 2 * j + 1)),
            pl.BlockSpec((3, tf), lambda i, j: (0, j)),
            pl.BlockSpec((1, tf), lambda i, j: (0, j)),
        ],
        out_specs=pl.BlockSpec((tm, tf), lambda i, j: (i, j)),
        compiler_params=_params(("parallel", "arbitrary"), vmem),
        name="ffn_up",
    )(h_tiles, w_ag, w_ag, conv_w, conv_b)


def _pad_cols(w, n):
    return jnp.pad(w, ((0, 0), (0, n - w.shape[1])))


def _prepare_weights(lambda_init, rel_bias, g_mix, w_in, q_norm_g, k_norm_g, lambda_q1, lambda_k1, lambda_q2,
                     lambda_k2, da_subln_g, w_gate_fwd, b_gate_fwd, w_gate_bwd, b_gate_bwd, gla_norm_g,
                     w_branch_a, w_branch_b, w_out, g_ffn, w_up, conv_w, conv_b, w_down):
    heads = rel_bias.shape[1]
    d = q_norm_g.shape[-1]
    da = heads * 2 * d
    rank, gla_k = w_gate_fwd.shape
    dv = gla_norm_g.shape[-1]
    gla_v = w_branch_b.shape[0]
    gh = gla_v // dv
    dk = gla_k // gh
    f = conv_b.shape[-1]
    d_model = w_in.shape[0]
    widths = [da, da, da, gla_k, gla_k, gla_v, gla_v, rank, rank, d_model, d_model]
    starts = [0]
    for wd in widths:
        starts.append(starts[-1] + wd)
    sec = lambda s: (starts[s], widths[s])

    w_stems = _pad_cols(w_in[:, starts[7]:starts[9]].astype(_BF16), dk)
    w_gates = w_in[:, starts[9]:].astype(_BF16)
    qk_gain = jnp.concatenate([jnp.tile(q_norm_g * (d ** -0.5 * _LOG2E), 2 * heads), jnp.tile(k_norm_g, 2 * heads)])

    def gate_weight(w, first_row):
        return jnp.zeros((dk, gla_k), _BF16).at[first_row:first_row + rank].set(w.astype(_BF16))

    fp = -(-f // _FFN_TILE) * _FFN_TILE
    down_tile = _tile(d_model, _DOWN_TILE, _LANES)
    return dict(
        heads=heads, d=d, gh=gh, dk=dk, dv=dv, lambda_init=lambda_init,
        rel_bias=rel_bias, g_mix=g_mix, g_ffn=g_ffn,
        w_in=w_in.astype(_BF16), sec_qk=(sec(0), sec(1)), sec_vgr=(sec(2), sec(5), sec(6)),
        sec_gla_qk=(sec(3), sec(4)),
        w_stems=w_stems, w_gates=w_gates, qk_gain=qk_gain.reshape(1, -1),
        lam_vecs=jnp.stack([lambda_q1, lambda_k1, lambda_q2, lambda_k2]),
        da_subln_g=da_subln_g, gla_norm_g=gla_norm_g,
        wg_fwd=gate_weight(w_gate_fwd, 0), wg_bwd=gate_weight(w_gate_bwd, rank),
        bg_fwd=b_gate_fwd.reshape(1, -1), bg_bwd=b_gate_bwd.reshape(1, -1),
        p_a=w_branch_a.astype(_BF16), p_b=w_branch_b.astype(_BF16), w_out=w_out.astype(_BF16),
        f=f, w_up=_ffn_up_weight(w_up, f),
        conv_w=_pad_cols(conv_w, fp), conv_b=_pad_cols(conv_b.reshape(1, f), fp),
        down_tile=down_tile,
        w_down=w_down.astype(_BF16).reshape(f, d_model // down_tile, down_tile).transpose(1, 0, 2),
    )


def _encoder_layer(x3, p, band, tq, tk):
    batch, seq, d_model = x3.shape
    x = x3.reshape(batch * seq, d_model)
    heads, d, gh, dk, dv = p["heads"], p["d"], p["gh"], p["dk"], p["dv"]

    h = _rmsnorm(x, p["g_mix"])
    qk = _project(functools.partial(_proj_groupnorm_body, group=d), h, p["w_in"], _BF16, p["sec_qk"],
                  extra=(p["qk_gain"],), extra_specs=(_row_vec_spec,), name="proj_qk")
    vgr = _project(_proj_cast_body, h, p["w_in"], _BF16, p["sec_vgr"], name="proj_vgr")
    gla_qk = _project(_proj_cast_body, h, p["w_in"], _F32, p["sec_gla_qk"], name="proj_gla_qk")
    stems = _project(_proj_cast_body, h, p["w_stems"], _F32, name="proj_stems")
    gates = _project(_proj_sigmoid_body, h, p["w_gates"], _BF16, name="proj_gates")

    o_a = _diff_attention(qk, vgr, band, p["lam_vecs"], p["da_subln_g"], batch, seq, heads, d,
                          p["lambda_init"], tq, tk)

    v_col0 = (heads * 2 * d) // dv
    r_col0 = v_col0 + gh
    fwd = _gla(gla_qk, stems, vgr, p["wg_fwd"], p["bg_fwd"], batch, seq, gh, dk, dv, v_col0, reverse=False)
    o_b = _gla(gla_qk, stems, vgr, p["wg_bwd"], p["bg_bwd"], batch, seq, gh, dk, dv, v_col0, reverse=True,
               fwd=fwd, r_col0=r_col0, norm_g=p["gla_norm_g"])

    merged = _merge(o_a, o_b, p["p_a"], p["p_b"], gates)
    x1 = _project(_proj_residual_body, merged, p["w_out"], _F32, extra=(x,), extra_specs=(_tile_spec,),
                  name="proj_out")

    tm = _tile(seq, 1024, 2 * _HALO)
    h2 = _rmsnorm_halo(x1, p["g_ffn"], seq, tm)
    act = _ffn_up(h2, p["w_up"], p["conv_w"], p["conv_b"], tm, p["f"])
    y = _project(_proj_residual_body, act, p["w_down"], _F32, extra=(x1,), extra_specs=(_tile_spec,),
                 tile=p["down_tile"], name="ffn_down")
    return y.reshape(batch, seq, d_model)


def kernel(x_prompt, x_sample, rel_bias, g_mix, w_in, q_norm_g, k_norm_g, lambda_q1, lambda_k1, lambda_q2,
           lambda_k2, da_subln_g, w_gate_fwd, b_gate_fwd, w_gate_bwd, b_gate_bwd, gla_norm_g, w_branch_a,
           w_branch_b, w_out, g_ffn, w_up, conv_w, conv_b, w_down):
    layer_weights = (g_mix, w_in, q_norm_g, k_norm_g, lambda_q1, lambda_k1, lambda_q2, lambda_k2, da_subln_g,
                     w_gate_fwd, b_gate_fwd, w_gate_bwd, b_gate_bwd, gla_norm_g, w_branch_a, w_branch_b, w_out,
                     g_ffn, w_up, conv_w, conv_b, w_down)
    min_seq = min(x_prompt.shape[1], x_sample.shape[1])
    tq = _tile(min_seq, 512, _LANES)
    tk = _tile(min_seq, 512, tq)
    band = _bias_band(rel_bias, tq, tk)
    y_prompt, y_sample = x_prompt, x_sample
    for l in range(g_mix.shape[0]):
        lambda_init = 0.8 - 0.6 * math.exp(-0.3 * l)
        p = _prepare_weights(lambda_init, rel_bias, *(w[l] for w in layer_weights))
        y_prompt = _encoder_layer(y_prompt, p, band, tq, tk)
        y_sample = _encoder_layer(y_sample, p, band, tq, tk)
    return (y_prompt, y_sample)
```

```python
import functools
import math

import jax
import jax.numpy as jnp
from jax import lax
from jax.experimental import pallas as pl
from jax.experimental.pallas import tpu as pltpu

_EPS = 1e-6
_GLA_CHUNK = 64
_GLA_TAU = 16.0
_REL_MAX_DIST = 128
_LOG2E = math.log2(math.e)
_LANES = 128
_HALO = 16
_ROW_GROUP = 16
_FFN_TILE = 512
_DOWN_TILE = 512
_V7X_VMEM_BYTES = 64 * 1024 * 1024
_VMEM_BUDGET = _V7X_VMEM_BYTES - 8 * 1024 * 1024

_F32 = jnp.float32
_BF16 = jnp.bfloat16
_NT = (((1,), (1,)), ((), ()))
_TN = (((0,), (0,)), ((), ()))


def _params(semantics, vmem_bytes):
    return pltpu.CompilerParams(dimension_semantics=semantics,
                                vmem_limit_bytes=int(min(vmem_bytes, _VMEM_BUDGET)))


def _tile(n, pref, align):
    t = min(pref, n)
    t -= t % align
    while t > align and n % t:
        t -= align
    assert t >= align and n % t == 0, (n, pref, align)
    return t


def _nbytes(shape, dtype):
    return math.prod(shape) * jnp.dtype(dtype).itemsize


def _dot(a, b):
    return jnp.dot(a, b, preferred_element_type=_F32)


def _rms(x, g):
    ms = jnp.mean(x * x, axis=-1, keepdims=True)
    return x * lax.rsqrt(ms + _EPS) * g


def _rmsnorm_body(x_ref, g_ref, o_ref):
    o_ref[...] = _rms(x_ref[...], g_ref[...]).astype(o_ref.dtype)


def _rmsnorm(x, g):
    m, d = x.shape
    tm = _tile(m, 256, 8)
    return pl.pallas_call(
        _rmsnorm_body,
        out_shape=jax.ShapeDtypeStruct((m, d), _BF16),
        grid=(m // tm,),
        in_specs=[pl.BlockSpec((tm, d), lambda i: (i, 0)), pl.BlockSpec((1, d), lambda i: (0, 0))],
        out_specs=pl.BlockSpec((tm, d), lambda i: (i, 0)),
        compiler_params=_params(("parallel",), 6 * _nbytes((tm, d), _F32) + (8 << 20)),
        name="rmsnorm",
    )(x, g.reshape(1, d))


def _rmsnorm_halo_body(x_ref, prev_ref, next_ref, g_ref, o_ref, *, tm, tr, tiles_per_seq):
    r = pl.program_id(1)
    row0 = pl.multiple_of(r * tr, tr)
    o_ref[pl.ds(row0, tr), :] = _rms(x_ref[...], g_ref[...]).astype(o_ref.dtype)

    @pl.when(r == 0)
    def _():
        pos = pl.program_id(0) % tiles_per_seq
        nxt = jnp.where(pos == tiles_per_seq - 1, 0.0, _rms(next_ref[...], g_ref[...]))
        prv = jnp.where(pos == 0, 0.0, _rms(prev_ref[...], g_ref[...]))
        o_ref[tm:tm + _HALO, :] = nxt.astype(o_ref.dtype)
        o_ref[tm + _HALO:tm + 2 * _HALO, :] = prv.astype(o_ref.dtype)


def _rmsnorm_halo(x, g, seq, tm):
    m, d = x.shape
    tr = _tile(tm, 256, _HALO)
    nt, nr = m // tm, tm // tr
    hb = tm // _HALO
    last = m // _HALO - 1
    return pl.pallas_call(
        functools.partial(_rmsnorm_halo_body, tm=tm, tr=tr, tiles_per_seq=seq // tm),
        out_shape=jax.ShapeDtypeStruct((nt, tm + 2 * _HALO, d), _BF16),
        grid=(nt, nr),
        in_specs=[
            pl.BlockSpec((tr, d), lambda i, r: (i * nr + r, 0)),
            pl.BlockSpec((_HALO, d), lambda i, r: (jnp.maximum(i * hb - 1, 0), 0)),
            pl.BlockSpec((_HALO, d), lambda i, r: (jnp.minimum((i + 1) * hb, last), 0)),
            pl.BlockSpec((1, d), lambda i, r: (0, 0)),
        ],
        out_specs=pl.BlockSpec((None, tm + 2 * _HALO, d), lambda i, r: (i, 0, 0)),
        compiler_params=_params(("parallel", "arbitrary"),
                                6 * _nbytes((tr, d), _F32) + 2 * _nbytes((tm + 2 * _HALO, d), _BF16) + (8 << 20)),
        name="rmsnorm_halo",
    )(x, x, x, g.reshape(1, d))


def _proj_cast_body(a_ref, b_ref, o_ref):
    o_ref[...] = _dot(a_ref[...], b_ref[...]).astype(o_ref.dtype)


def _proj_sigmoid_body(a_ref, b_ref, o_ref):
    o_ref[...] = (0.5 * jnp.tanh(0.5 * _dot(a_ref[...], b_ref[...])) + 0.5).astype(o_ref.dtype)


def _proj_groupnorm_body(a_ref, b_ref, g_ref, o_ref, *, group):
    acc = _dot(a_ref[...], b_ref[...])
    for c in range(acc.shape[1] // group):
        cols = slice(c * group, (c + 1) * group)
        o_ref[:, cols] = _rms(acc[:, cols], g_ref[:, cols]).astype(o_ref.dtype)


def _proj_residual_body(a_ref, b_ref, r_ref, o_ref):
    o_ref[...] = r_ref[...] + _dot(a_ref[...], b_ref[...])


def _proj_residual_halves_body(a_ref, b_top_ref, b_bot_ref, r_ref, o_ref):
    kh = b_top_ref.shape[0]
    o_ref[...] = r_ref[...] + (_dot(a_ref[:, 0:kh], b_top_ref[...]) + _dot(a_ref[:, kh:], b_bot_ref[...]))


def _project(body, a, b, out_dtype, sections=None, extra=(), extra_specs=(), tile=1024, b_halves=False,
             name="project"):
    m, k = a.shape
    sections = sections or ((0, b.shape[1]),)
    n = sum(width for _, width in sections)
    tm = _tile(m, tile, 16)
    tn = _tile(math.gcd(*(v for sec in sections for v in sec)), tile, _LANES)
    first_out, shifts, done = [], [], 0
    for start, width in sections:
        first_out.append(done // tn)
        shifts.append(start // tn - done // tn)
        done += width

    def src_block(j):
        shift = shifts[0]
        for first, s in zip(first_out[1:], shifts[1:]):
            shift = jnp.where(j >= first, s, shift)
        return j + shift

    vmem = (2 * _nbytes((tm, k), a.dtype) + 2 * _nbytes((k, tn), b.dtype)
            + 6 * _nbytes((tm, tn), _F32) + (4 << 20))
    if b_halves:
        b_specs = [pl.BlockSpec((k // 2, tn), lambda i, j, half=half: (half, src_block(j))) for half in range(2)]
    else:
        b_specs = [pl.BlockSpec((k, tn), lambda i, j: (0, src_block(j)))]
    return pl.pallas_call(
        body,
        out_shape=jax.ShapeDtypeStruct((m, n), out_dtype),
        grid=(m // tm, n // tn),
        in_specs=[pl.BlockSpec((tm, k), lambda i, j: (i, 0)), *b_specs,
                  *[spec(tm, tn) for spec in extra_specs]],
        out_specs=pl.BlockSpec((tm, tn), lambda i, j: (i, j)),
        compiler_params=_params(("parallel", "arbitrary"), vmem),
        name=name,
    )(a, *[b] * len(b_specs), *extra)


def _row_vec_spec(tm, tn):
    return pl.BlockSpec((1, tn), lambda i, j: (0, j))


def _tile_spec(tm, tn):
    return pl.BlockSpec((tm, tn), lambda i, j: (i, j))


def _t5_bucket(rel, num_buckets):
    half = num_buckets // 2
    max_exact = half // 2
    ret = jnp.where(rel > 0, half, 0)
    n = jnp.abs(rel)
    nf = jnp.maximum(n, 1).astype(_F32)
    large = max_exact + (jnp.log(nf / max_exact) / math.log(_REL_MAX_DIST / max_exact)
                         * (half - max_exact)).astype(jnp.int32)
    large = jnp.minimum(large, half - 1)
    return ret + jnp.where(n < max_exact, n, large)


def _band_body(rb_ref, o_ref, *, tq, tk, m_lo, num_buckets):
    h = pl.program_id(0)
    mi = pl.program_id(1)
    last = pl.num_programs(1) - 1

    @pl.when(mi == 0)
    def _():
        o_ref[...] = jnp.full((tq, tk), rb_ref[h, num_buckets // 2 - 1] * _LOG2E, _F32)

    @pl.when(mi == last)
    def _():
        o_ref[...] = jnp.full((tq, tk), rb_ref[h, num_buckets - 1] * _LOG2E, _F32)

    @pl.when((mi > 0) & (mi < last))
    def _():
        row = lax.broadcasted_iota(jnp.int32, (tq, tk), 0)
        col = lax.broadcasted_iota(jnp.int32, (tq, tk), 1)
        bucket = _t5_bucket((mi + m_lo) * tq + col - row, num_buckets)
        out = jnp.zeros((tq, tk), _F32)
        for b in range(num_buckets):
            out = jnp.where(bucket == b, rb_ref[h, b], out)
        o_ref[...] = out * _LOG2E


def _band_range(tq, tk):
    m_lo = -((_REL_MAX_DIST - 1 + tk + tq - 1) // tq)
    m_hi = (_REL_MAX_DIST - 1 + tq + tq - 1) // tq
    return m_lo, m_hi


def _bias_band(rel_bias, tq, tk):
    nbk, h = rel_bias.shape
    m_lo, m_hi = _band_range(tq, tk)
    nb = m_hi - m_lo + 1
    return pl.pallas_call(
        functools.partial(_band_body, tq=tq, tk=tk, m_lo=m_lo, num_buckets=nbk),
        out_shape=jax.ShapeDtypeStruct((h, nb, tq, tk), _F32),
        grid=(h, nb),
        in_specs=[pl.BlockSpec(memory_space=pltpu.SMEM)],
        out_specs=pl.BlockSpec((None, None, tq, tk), lambda i, j: (i, j, 0, 0)),
        compiler_params=_params(("parallel", "parallel"), 24 * _nbytes((tq, tk), _F32) + (4 << 20)),
        name="bias_band",
    )(rel_bias.T)


def _lane_fold(x, op):
    return functools.reduce(op, [x[:, c:c + _LANES] for c in range(0, x.shape[1], _LANES)])


def _attn_body(lam_ref, q_ref, k_ref, v_ref, band_ref, g_ref, o_ref, s_ref, p_ref,
               *, tq, tk, d, m_lo, m_hi, lambda_init):
    i = pl.program_id(2)
    nk = k_ref.shape[0] // tk
    ratio = tk // tq

    band_idx = [jnp.clip(j * ratio - i, m_lo, m_hi) - m_lo for j in range(nk)]

    def scores(mp):
        cols = slice(mp * d, (mp + 1) * d)
        qm = q_ref[:, cols]
        for j in range(nk):
            keys = slice(j * tk, (j + 1) * tk)
            s_ref[mp, :, keys] = lax.dot_general(qm, k_ref[keys, cols], _NT, preferred_element_type=_F32)

    def softmax_rows(mp):
        sums = []
        for r in range(0, tq, _ROW_GROUP):
            rows = slice(r, r + _ROW_GROUP)
            part_max = None
            for j in range(nk):
                keys = slice(j * tk, (j + 1) * tk)
                folded = _lane_fold(s_ref[mp, rows, keys] + band_ref[band_idx[j], rows, :], jnp.maximum)
                part_max = folded if part_max is None else jnp.maximum(part_max, folded)
            row_max = jnp.max(part_max, axis=-1, keepdims=True)
            part_sum = jnp.zeros((_ROW_GROUP, _LANES), _F32)
            for j in range(nk):
                keys = slice(j * tk, (j + 1) * tk)
                p = jnp.exp2(s_ref[mp, rows, keys] + (band_ref[band_idx[j], rows, :] - row_max))
                part_sum = part_sum + _lane_fold(p, jnp.add)
                p_ref[mp, rows, keys] = p.astype(p_ref.dtype)
            sums.append(jnp.sum(part_sum, axis=-1, keepdims=True))
        return jnp.concatenate(sums, axis=0)

    scores(0)
    scores(1)
    heads_out = []
    for mp in range(2):
        row_sum = softmax_rows(mp)
        heads_out.append(_dot(p_ref[mp], v_ref[...]) / row_sum)

    lv = lam_ref[...]
    lam = (jnp.exp(jnp.sum(lv[0:1] * lv[1:2], axis=-1, keepdims=True))
           - jnp.exp(jnp.sum(lv[2:3] * lv[3:4], axis=-1, keepdims=True)) + lambda_init)
    o = heads_out[0] - lam * heads_out[1]
    o_ref[...] = (_rms(o, g_ref[...]) * (1.0 - lambda_init)).astype(o_ref.dtype)


def _diff_attention(qk, v_src, band, lam_vecs, subln_g, batch, seq, heads, d, lambda_init, tq, tk):
    m = qk.shape[0]
    nq = seq // tq
    m_lo, m_hi = _band_range(tq, tk)
    nb = m_hi - m_lo + 1
    w = 2 * d
    vmem = (4 * _nbytes((seq, w), _BF16) + 2 * _nbytes((nb, tq, tk), _F32) + _nbytes((2, tq, seq), _F32)
            + _nbytes((2, tq, seq), _BF16) + 8 * _nbytes((tq, tk), _F32) + (6 << 20))
    return pl.pallas_call(
        functools.partial(_attn_body, tq=tq, tk=tk, d=d, m_lo=m_lo, m_hi=m_hi, lambda_init=lambda_init),
        out_shape=jax.ShapeDtypeStruct((m, heads * w), _BF16),
        grid=(batch, heads, nq),
        in_specs=[
            pl.BlockSpec((4, d), lambda b, h, i: (0, 0)),
            pl.BlockSpec((tq, w), lambda b, h, i: (b * nq + i, h)),
            pl.BlockSpec((seq, w), lambda b, h, i: (b, heads + h)),
            pl.BlockSpec((seq, w), lambda b, h, i: (b, h)),
            pl.BlockSpec((None, nb, tq, tk), lambda b, h, i: (h, 0, 0, 0)),
            pl.BlockSpec((1, w), lambda b, h, i: (0, 0)),
        ],
        out_specs=pl.BlockSpec((tq, w), lambda b, h, i: (b * nq + i, h)),
        scratch_shapes=[pltpu.VMEM((2, tq, seq), _F32), pltpu.VMEM((2, tq, seq), _BF16)],
        compiler_params=_params(("parallel", "parallel", "arbitrary"), vmem),
        name="diff_attention",
    )(lam_vecs, qk, qk, v_src, band, subln_g.reshape(1, w))


def _split(x):
    hi = x.astype(_BF16)
    return hi, (x - hi.astype(_F32)).astype(_BF16)


def _split_dot(lhs, parts, dims=None):
    if dims is None:
        return sum(_dot(lhs, part) for part in parts)
    return sum(lax.dot_general(part, lhs, dims, preferred_element_type=_F32) for part in parts)


def _gla_body(*refs, tt, dk, dv, hpb, reverse, final, q_scale):
    if final:
        (q_ref, k_ref, lr_ref, v_ref, wg_ref, bg_ref, fwd_ref, r_ref, gn_ref, o_ref, s_ref) = refs
    else:
        (q_ref, k_ref, lr_ref, v_ref, wg_ref, bg_ref, o_ref, s_ref) = refs
    c_len = _GLA_CHUNK
    nc = tt // c_len

    @pl.when(pl.program_id(2) == 0)
    def _():
        s_ref[...] = jnp.zeros_like(s_ref)

    row = lax.broadcasted_iota(jnp.int32, (tt, tt), 0)
    col = lax.broadcasted_iota(jnp.int32, (tt, tt), 1)
    causal = ((row // c_len) == (col // c_len)) & ((col >= row) if reverse else (col <= row))
    causal_ones = causal.astype(_BF16)
    ones_cols = jnp.ones((tt, _LANES), _BF16)
    stems = lr_ref[...].astype(_BF16)
    order = list(reversed(range(nc))) if reverse else list(range(nc))
    edge = 0 if reverse else c_len - 1

    def chunk_rows(x, c):
        return x[c * c_len:(c + 1) * c_len]

    def per_chunk(fn):
        return jnp.concatenate([fn(c) for c in range(nc)], axis=0)

    for hh in range(hpb):
        kc = slice(hh * dk, (hh + 1) * dk)
        vc = slice(hh * dv, (hh + 1) * dv)
        v = v_ref[:, vc]
        pre = _dot(stems, wg_ref[:, kc]) + bg_ref[:, kc]
        log_a = (jnp.minimum(pre, 0.0) - jnp.log(1.0 + jnp.exp(-jnp.abs(pre)))) * (_LOG2E / _GLA_TAU)
        log_a_parts = _split(log_a)
        b = _split_dot(causal_ones, log_a_parts)
        total = [chunk_rows(b, c)[edge:edge + 1] for c in range(nc)]
        start, run = {}, jnp.zeros((1, dk), _F32)
        for c in order:
            start[c] = run
            run = run + total[c]

        q_own = q_ref[:, kc] * q_scale * jnp.exp2(b)
        k_own = k_ref[:, kc] * jnp.exp2(per_chunk(lambda c: total[c] - chunk_rows(b, c)))
        q_dec = q_own.astype(_BF16)
        k_inv = (k_ref[:, kc] * jnp.exp2(-b)).astype(_BF16)
        att = jnp.where(causal, lax.dot_general(q_dec, k_inv, _NT, preferred_element_type=_F32), 0.0)

        def seen_keys(c):
            pos = order.index(c)
            return per_chunk(lambda c2: (chunk_rows(k_own, c2) * jnp.exp2(start[c] - start[c2] - total[c2])
                                         if order.index(c2) < pos else jnp.zeros((c_len, dk), _F32)))

        att = att + per_chunk(lambda c: (
            jnp.zeros((c_len, tt), _F32) if c == order[0] else
            lax.dot_general(chunk_rows(q_dec, c), seen_keys(c).astype(_BF16), _NT, preferred_element_type=_F32)))

        s = s_ref[hh]
        q_state = per_chunk(lambda c: chunk_rows(q_own, c) * jnp.exp2(start[c])).astype(_BF16)
        k_state = per_chunk(lambda c: chunk_rows(k_own, c) * jnp.exp2(run - start[c] - total[c])).astype(_BF16)
        o = _dot(att.astype(_BF16), v) + _dot(q_state, s.astype(_BF16))
        tile_decay = jnp.exp2(_split_dot(ones_cols, log_a_parts, _TN))[:, 0:1]
        s_ref[hh] = tile_decay * s + lax.dot_general(k_state, v, _TN, preferred_element_type=_F32)
        if final:
            o = _rms(o + fwd_ref[:, vc], gn_ref[...])
            r = r_ref[:, vc].astype(_F32)
            o_ref[:, vc] = (o * (r * (0.5 * jnp.tanh(0.5 * r) + 0.5))).astype(o_ref.dtype)
        else:
            o_ref[:, vc] = o


def _gla(qk, stems, vgr, w_gate, b_gate, batch, seq, heads, dk, dv, v_col0, reverse,
         fwd=None, r_col0=None, norm_g=None):
    m = qk.shape[0]
    tt = _tile(seq, 256, _GLA_CHUNK)
    nt = seq // tt
    hpb = _tile(heads, 4, 1)
    ng = heads // hpb
    final = fwd is not None
    assert v_col0 % hpb == 0 and (r_col0 is None or r_col0 % hpb == 0)

    def rows(b, t):
        return b * nt + (nt - 1 - t if reverse else t)

    in_specs = [
        pl.BlockSpec((tt, hpb * dk), lambda b, g, t: (rows(b, t), g)),
        pl.BlockSpec((tt, hpb * dk), lambda b, g, t: (rows(b, t), ng + g)),
        pl.BlockSpec((tt, dk), lambda b, g, t: (rows(b, t), 0)),
        pl.BlockSpec((tt, hpb * dv), lambda b, g, t: (rows(b, t), v_col0 // hpb + g)),
        pl.BlockSpec((dk, hpb * dk), lambda b, g, t: (0, g)),
        pl.BlockSpec((1, hpb * dk), lambda b, g, t: (0, g)),
    ]
    args = [qk, qk, stems, vgr, w_gate, b_gate]
    if final:
        in_specs += [
            pl.BlockSpec((tt, hpb * dv), lambda b, g, t: (rows(b, t), g)),
            pl.BlockSpec((tt, hpb * dv), lambda b, g, t: (rows(b, t), r_col0 // hpb + g)),
            pl.BlockSpec((1, dv), lambda b, g, t: (0, 0)),
        ]
        args += [fwd, vgr, norm_g.reshape(1, dv)]
    return pl.pallas_call(
        functools.partial(_gla_body, tt=tt, dk=dk, dv=dv, hpb=hpb, reverse=reverse, final=final,
                          q_scale=dk ** -0.5),
        out_shape=jax.ShapeDtypeStruct((m, heads * dv), _BF16 if final else _F32),
        grid=(batch, ng, nt),
        in_specs=in_specs,
        out_specs=pl.BlockSpec((tt, hpb * dv), lambda b, g, t: (rows(b, t), g)),
        scratch_shapes=[pltpu.VMEM((hpb, dk, dv), _F32)],
        compiler_params=_params(("parallel", "parallel", "arbitrary"), 48 << 20),
        name="gla_bwd" if reverse else "gla_fwd",
    )(*args)


def _merge_body(oa_ref, ob_ref, pa_ref, pb_ref, ga_ref, gb_ref, o_ref):
    a = _dot(oa_ref[...], pa_ref[...])
    b = _dot(ob_ref[...], pb_ref[...])
    o_ref[...] = (ga_ref[...].astype(_F32) * a + gb_ref[...].astype(_F32) * b).astype(o_ref.dtype)


def _merge(o_a, o_b, p_a, p_b, gates):
    m, ka = o_a.shape
    kb = o_b.shape[1]
    n = p_a.shape[1]
    tm = _tile(m, 1024, 16)
    tn = _tile(n, 1024, _LANES)
    nj = n // tn
    vmem = (2 * _nbytes((tm, ka + kb), _BF16) + 2 * _nbytes((ka + kb, tn), _BF16)
            + 4 * _nbytes((tm, tn), _BF16) + 6 * _nbytes((tm, tn), _F32) + (4 << 20))
    return pl.pallas_call(
        _merge_body,
        out_shape=jax.ShapeDtypeStruct((m, n), _BF16),
        grid=(m // tm, nj),
        in_specs=[
            pl.BlockSpec((tm, ka), lambda i, j: (i, 0)),
            pl.BlockSpec((tm, kb), lambda i, j: (i, 0)),
            pl.BlockSpec((ka, tn), lambda i, j: (0, j)),
            pl.BlockSpec((kb, tn), lambda i, j: (0, j)),
            pl.BlockSpec((tm, tn), lambda i, j: (i, j)),
            pl.BlockSpec((tm, tn), lambda i, j: (i, nj + j)),
        ],
        out_specs=pl.BlockSpec((tm, tn), lambda i, j: (i, j)),
        compiler_params=_params(("parallel", "arbitrary"), vmem),
        name="branch_merge",
    )(o_a, o_b, p_a, p_b, gates, gates)


def _gelu_tanh(x):
    return x * (0.5 * (1.0 + jnp.tanh(math.sqrt(2.0 / math.pi) * (x + 0.044715 * (x * x * x)))))


def _ffn_up_weight_body(w_ref, o_ref, *, nblk):
    t = pl.program_id(0)
    q = 2 * (t // 4) + t % 2
    o_ref[...] = jnp.where(q < nblk, w_ref[...], 0.0).astype(o_ref.dtype)


def _ffn_up_weight(w_up, f):
    d = w_up.shape[0]
    blk = _FFN_TILE // 2
    assert f % blk == 0
    nblk = f // blk
    n_out = 4 * (-(-f // _FFN_TILE))

    def src(t):
        q = 2 * (t // 4) + t % 2
        return 0, jnp.minimum(jnp.where(t % 4 < 2, q, nblk + q), 2 * nblk - 1)

    return pl.pallas_call(
        functools.partial(_ffn_up_weight_body, nblk=nblk),
        out_shape=jax.ShapeDtypeStruct((d, n_out * blk), _BF16),
        grid=(n_out,),
        in_specs=[pl.BlockSpec((d, blk), src)],
        out_specs=pl.BlockSpec((d, blk), lambda t: (0, t)),
        compiler_params=_params(("parallel",), 6 * _nbytes((d, blk), _F32) + (4 << 20)),
        name="ffn_up_weight",
    )(w_up)


def _ffn_up_body(h_ref, wa_ref, wb_ref, cw_ref, cb_ref, o_ref, *, tm, last_width):
    def gated(width):
        cols = slice(0, width)
        a = _dot(h_ref[...], wa_ref[:, cols])
        g = _dot(h_ref[0:tm, :], wb_ref[:, cols])
        rows = a.shape[0]
        prev = pltpu.roll(a, 1, 0)[0:tm]
        nxt = pltpu.roll(a, rows - 1, 0)[0:tm]
        conv = prev * cw_ref[0:1, cols] + a[0:tm] * cw_ref[1:2, cols] + nxt * cw_ref[2:3, cols] + cb_ref[:, cols]
        o_ref[:, cols] = (_gelu_tanh(conv) * g).astype(o_ref.dtype)

    tf = o_ref.shape[1]
    if last_width == tf:
        gated(tf)
    else:
        is_last = pl.program_id(1) == pl.num_programs(1) - 1
        pl.when(jnp.logical_not(is_last))(functools.partial(gated, tf))
        pl.when(is_last)(functools.partial(gated, last_width))


def _ffn_up(h_tiles, w_ag, conv_w, conv_b, tm, f):
    nt, rows, d = h_tiles.shape
    tf = _FFN_TILE
    vmem = (2 * _nbytes((rows, d), _BF16) + 4 * _nbytes((d, tf), _BF16) + 2 * _nbytes((tm, tf), _BF16)
            + 8 * _nbytes((rows, tf), _F32) + (4 << 20))
    return pl.pallas_call(
        functools.partial(_ffn_up_body, tm=tm, last_width=f % tf or tf),
        out_shape=jax.ShapeDtypeStruct((nt * tm, f), _BF16),
        grid=(nt, w_ag.shape[1] // (2 * tf)),
        in_specs=[
            pl.BlockSpec((None, rows, d), lambda i, j: (i, 0, 0)),
            pl.BlockSpec((d, tf), lambda i, j: (0, 2 * j)),
            pl.BlockSpec((d, tf), lambda i, j: (0, 2 * j + 1)),
            pl.BlockSpec((3, tf), lambda i, j: (0, j)),
            pl.BlockSpec((1, tf), lambda i, j: (0, j)),
        ],
        out_specs=pl.BlockSpec((tm, tf), lambda i, j: (i, j)),
        compiler_params=_params(("parallel", "arbitrary"), vmem),
        name="ffn_up",
    )(h_tiles, w_ag, w_ag, conv_w, conv_b)


def _pad_cols(w, n):
    return jnp.pad(w, ((0, 0), (0, n - w.shape[1])))


def _prepare_weights(lambda_init, rel_bias, g_mix, w_in, q_norm_g, k_norm_g, lambda_q1, lambda_k1, lambda_q2,
                     lambda_k2, da_subln_g, w_gate_fwd, b_gate_fwd, w_gate_bwd, b_gate_bwd, gla_norm_g,
                     w_branch_a, w_branch_b, w_out, g_ffn, w_up, conv_w, conv_b, w_down):
    heads = rel_bias.shape[1]
    d = q_norm_g.shape[-1]
    da = heads * 2 * d
    rank, gla_k = w_gate_fwd.shape
    dv = gla_norm_g.shape[-1]
    gla_v = w_branch_b.shape[0]
    gh = gla_v // dv
    dk = gla_k // gh
    f = conv_b.shape[-1]
    d_model = w_in.shape[0]
    widths = [da, da, da, gla_k, gla_k, gla_v, gla_v, rank, rank, d_model, d_model]
    starts = [0]
    for wd in widths:
        starts.append(starts[-1] + wd)
    sec = lambda s: (starts[s], widths[s])

    w_stems = _pad_cols(w_in[:, starts[7]:starts[9]].astype(_BF16), dk)
    w_gates = w_in[:, starts[9]:].astype(_BF16)
    qk_gain = jnp.concatenate([jnp.tile(q_norm_g * (d ** -0.5 * _LOG2E), 2 * heads), jnp.tile(k_norm_g, 2 * heads)])

    def gate_weight(w, first_row):
        return jnp.zeros((dk, gla_k), _BF16).at[first_row:first_row + rank].set(w.astype(_BF16))

    fp = -(-f // _FFN_TILE) * _FFN_TILE
    return dict(
        heads=heads, d=d, gh=gh, dk=dk, dv=dv, lambda_init=lambda_init,
        rel_bias=rel_bias, g_mix=g_mix, g_ffn=g_ffn,
        w_in=w_in.astype(_BF16), sec_qk=(sec(0), sec(1)), sec_vgr=(sec(2), sec(5), sec(6)),
        sec_gla_qk=(sec(3), sec(4)),
        w_stems=w_stems, w_gates=w_gates, qk_gain=qk_gain.reshape(1, -1),
        lam_vecs=jnp.stack([lambda_q1, lambda_k1, lambda_q2, lambda_k2]),
        da_subln_g=da_subln_g, gla_norm_g=gla_norm_g,
        wg_fwd=gate_weight(w_gate_fwd, 0), wg_bwd=gate_weight(w_gate_bwd, rank),
        bg_fwd=b_gate_fwd.reshape(1, -1), bg_bwd=b_gate_bwd.reshape(1, -1),
        p_a=w_branch_a.astype(_BF16), p_b=w_branch_b.astype(_BF16), w_out=w_out.astype(_BF16),
        f=f, w_up=_ffn_up_weight(w_up, f),
        conv_w=_pad_cols(conv_w, fp), conv_b=_pad_cols(conv_b.reshape(1, f), fp),
        w_down=w_down.astype(_BF16),
    )


def _encoder_layer(x3, p, band, tq, tk):
    batch, seq, d_model = x3.shape
    x = x3.reshape(batch * seq, d_model)
    heads, d, gh, dk, dv = p["heads"], p["d"], p["gh"], p["dk"], p["dv"]

    h = _rmsnorm(x, p["g_mix"])
    qk = _project(functools.partial(_proj_groupnorm_body, group=d), h, p["w_in"], _BF16, p["sec_qk"],
                  extra=(p["qk_gain"],), extra_specs=(_row_vec_spec,), name="proj_qk")
    vgr = _project(_proj_cast_body, h, p["w_in"], _BF16, p["sec_vgr"], name="proj_vgr")
    gla_qk = _project(_proj_cast_body, h, p["w_in"], _F32, p["sec_gla_qk"], name="proj_gla_qk")
    stems = _project(_proj_cast_body, h, p["w_stems"], _F32, name="proj_stems")
    gates = _project(_proj_sigmoid_body, h, p["w_gates"], _BF16, name="proj_gates")

    o_a = _diff_attention(qk, vgr, band, p["lam_vecs"], p["da_subln_g"], batch, seq, heads, d,
                          p["lambda_init"], tq, tk)

    v_col0 = (heads * 2 * d) // dv
    r_col0 = v_col0 + gh
    fwd = _gla(gla_qk, stems, vgr, p["wg_fwd"], p["bg_fwd"], batch, seq, gh, dk, dv, v_col0, reverse=False)
    o_b = _gla(gla_qk, stems, vgr, p["wg_bwd"], p["bg_bwd"], batch, seq, gh, dk, dv, v_col0, reverse=True,
               fwd=fwd, r_col0=r_col0, norm_g=p["gla_norm_g"])

    merged = _merge(o_a, o_b, p["p_a"], p["p_b"], gates)
    x1 = _project(_proj_residual_body, merged, p["w_out"], _F32, extra=(x,), extra_specs=(_tile_spec,),
                  name="proj_out")

    tm = _tile(seq, 1024, 2 * _HALO)
    h2 = _rmsnorm_halo(x1, p["g_ffn"], seq, tm)
    act = _ffn_up(h2, p["w_up"], p["conv_w"], p["conv_b"], tm, p["f"])
    y = _project(_proj_residual_halves_body, act, p["w_down"], _F32, extra=(x1,), extra_specs=(_tile_spec,),
                 tile=_DOWN_TILE, b_halves=True, name="ffn_down")
    return y.reshape(batch, seq, d_model)


def kernel(x_prompt, x_sample, rel_bias, g_mix, w_in, q_norm_g, k_norm_g, lambda_q1, lambda_k1, lambda_q2,
           lambda_k2, da_subln_g, w_gate_fwd, b_gate_fwd, w_gate_bwd, b_gate_bwd, gla_norm_g, w_branch_a,
           w_branch_b, w_out, g_ffn, w_up, conv_w, conv_b, w_down):
    layer_weights = (g_mix, w_in, q_norm_g, k_norm_g, lambda_q1, lambda_k1, lambda_q2, lambda_k2, da_subln_g,
                     w_gate_fwd, b_gate_fwd, w_gate_bwd, b_gate_bwd, gla_norm_g, w_branch_a, w_branch_b, w_out,
                     g_ffn, w_up, conv_w, conv_b, w_down)
    min_seq = min(x_prompt.shape[1], x_sample.shape[1])
    tq = _tile(min_seq, 512, _LANES)
    tk = _tile(min_seq, 512, tq)
    band = _bias_band(rel_bias, tq, tk)
    y_prompt, y_sample = x_prompt, x_sample
    for l in range(g_mix.shape[0]):
        lambda_init = 0.8 - 0.6 * math.exp(-0.3 * l)
        p = _prepare_weights(lambda_init, rel_bias, *(w[l] for w in layer_weights))
        y_prompt = _encoder_layer(y_prompt, p, band, tq, tk)
        y_sample = _encoder_layer(y_sample, p, band, tq, tk)
    return (y_prompt, y_sample)
```

```python
import functools
import math

import jax
import jax.numpy as jnp
from jax import lax
from jax.experimental import pallas as pl
from jax.experimental.pallas import tpu as pltpu

_EPS = 1e-6
_GLA_CHUNK = 64
_GLA_TAU = 16.0
_REL_MAX_DIST = 128
_LOG2E = math.log2(math.e)
_LANES = 128
_HALO = 16
_ROW_GROUP = 16
_FFN_TILE = 512
_DOWN_TILE = 512
_V7X_VMEM_BYTES = 64 * 1024 * 1024
_VMEM_BUDGET = _V7X_VMEM_BYTES - 8 * 1024 * 1024

_F32 = jnp.float32
_BF16 = jnp.bfloat16
_NT = (((1,), (1,)), ((), ()))
_TN = (((0,), (0,)), ((), ()))


def _params(semantics, vmem_bytes):
    return pltpu.CompilerParams(dimension_semantics=semantics,
                                vmem_limit_bytes=int(min(vmem_bytes, _VMEM_BUDGET)))


def _tile(n, pref, align):
    t = min(pref, n)
    t -= t % align
    while t > align and n % t:
        t -= align
    assert t >= align and n % t == 0, (n, pref, align)
    return t


def _nbytes(shape, dtype):
    return math.prod(shape) * jnp.dtype(dtype).itemsize


def _dot(a, b):
    return jnp.dot(a, b, preferred_element_type=_F32)


def _rms(x, g):
    ms = jnp.mean(x * x, axis=-1, keepdims=True)
    return x * lax.rsqrt(ms + _EPS) * g


def _rmsnorm_body(x_ref, g_ref, o_ref):
    o_ref[...] = _rms(x_ref[...], g_ref[...]).astype(o_ref.dtype)


def _rmsnorm(x, g):
    m, d = x.shape
    tm = _tile(m, 256, 8)
    return pl.pallas_call(
        _rmsnorm_body,
        out_shape=jax.ShapeDtypeStruct((m, d), _BF16),
        grid=(m // tm,),
        in_specs=[pl.BlockSpec((tm, d), lambda i: (i, 0)), pl.BlockSpec((1, d), lambda i: (0, 0))],
        out_specs=pl.BlockSpec((tm, d), lambda i: (i, 0)),
        compiler_params=_params(("parallel",), 6 * _nbytes((tm, d), _F32) + (8 << 20)),
        name="rmsnorm",
    )(x, g.reshape(1, d))


def _rmsnorm_halo_body(x_ref, prev_ref, next_ref, g_ref, o_ref, *, tm, tr, tiles_per_seq):
    r = pl.program_id(1)
    row0 = pl.multiple_of(r * tr, tr)
    o_ref[pl.ds(row0, tr), :] = _rms(x_ref[...], g_ref[...]).astype(o_ref.dtype)

    @pl.when(r == 0)
    def _():
        pos = pl.program_id(0) % tiles_per_seq
        nxt = jnp.where(pos == tiles_per_seq - 1, 0.0, _rms(next_ref[...], g_ref[...]))
        prv = jnp.where(pos == 0, 0.0, _rms(prev_ref[...], g_ref[...]))
        o_ref[tm:tm + _HALO, :] = nxt.astype(o_ref.dtype)
        o_ref[tm + _HALO:tm + 2 * _HALO, :] = prv.astype(o_ref.dtype)


def _rmsnorm_halo(x, g, seq, tm):
    m, d = x.shape
    tr = _tile(tm, 256, _HALO)
    nt, nr = m // tm, tm // tr
    hb = tm // _HALO
    last = m // _HALO - 1
    return pl.pallas_call(
        functools.partial(_rmsnorm_halo_body, tm=tm, tr=tr, tiles_per_seq=seq // tm),
        out_shape=jax.ShapeDtypeStruct((nt, tm + 2 * _HALO, d), _BF16),
        grid=(nt, nr),
        in_specs=[
            pl.BlockSpec((tr, d), lambda i, r: (i * nr + r, 0)),
            pl.BlockSpec((_HALO, d), lambda i, r: (jnp.maximum(i * hb - 1, 0), 0)),
            pl.BlockSpec((_HALO, d), lambda i, r: (jnp.minimum((i + 1) * hb, last), 0)),
            pl.BlockSpec((1, d), lambda i, r: (0, 0)),
        ],
        out_specs=pl.BlockSpec((None, tm + 2 * _HALO, d), lambda i, r: (i, 0, 0)),
        compiler_params=_params(("parallel", "arbitrary"),
                                6 * _nbytes((tr, d), _F32) + 2 * _nbytes((tm + 2 * _HALO, d), _BF16) + (8 << 20)),
        name="rmsnorm_halo",
    )(x, x, x, g.reshape(1, d))


def _proj_cast_body(a_ref, b_ref, o_ref):
    o_ref[...] = _dot(a_ref[...], b_ref[...]).astype(o_ref.dtype)


def _proj_sigmoid_body(a_ref, b_ref, o_ref):
    o_ref[...] = (0.5 * jnp.tanh(0.5 * _dot(a_ref[...], b_ref[...])) + 0.5).astype(o_ref.dtype)


def _proj_groupnorm_body(a_ref, b_ref, g_ref, o_ref, *, group):
    acc = _dot(a_ref[...], b_ref[...])
    for c in range(acc.shape[1] // group):
        cols = slice(c * group, (c + 1) * group)
        o_ref[:, cols] = _rms(acc[:, cols], g_ref[:, cols]).astype(o_ref.dtype)


def _proj_residual_body(a_ref, b_ref, r_ref, o_ref):
    o_ref[...] = r_ref[...] + _dot(a_ref[...], b_ref[...])


def _project(body, a, b, out_dtype, sections=None, extra=(), extra_specs=(), tile=1024, name="project"):
    m, k = a.shape
    sections = sections or ((0, b.shape[1]),)
    n = sum(width for _, width in sections)
    tm = _tile(m, tile, 16)
    tn = _tile(math.gcd(*(v for sec in sections for v in sec)), tile, _LANES)
    first_out, shifts, done = [], [], 0
    for start, width in sections:
        first_out.append(done // tn)
        shifts.append(start // tn - done // tn)
        done += width

    def src_block(j):
        shift = shifts[0]
        for first, s in zip(first_out[1:], shifts[1:]):
            shift = jnp.where(j >= first, s, shift)
        return j + shift

    vmem = (2 * _nbytes((tm, k), a.dtype) + 2 * _nbytes((k, tn), b.dtype)
            + 6 * _nbytes((tm, tn), _F32) + (4 << 20))
    return pl.pallas_call(
        body,
        out_shape=jax.ShapeDtypeStruct((m, n), out_dtype),
        grid=(m // tm, n // tn),
        in_specs=[pl.BlockSpec((tm, k), lambda i, j: (i, 0)),
                  pl.BlockSpec((k, tn), lambda i, j: (0, src_block(j))),
                  *[spec(tm, tn) for spec in extra_specs]],
        out_specs=pl.BlockSpec((tm, tn), lambda i, j: (i, j)),
        compiler_params=_params(("parallel", "arbitrary"), vmem),
        name=name,
    )(a, b, *extra)


def _row_vec_spec(tm, tn):
    return pl.BlockSpec((1, tn), lambda i, j: (0, j))


def _tile_spec(tm, tn):
    return pl.BlockSpec((tm, tn), lambda i, j: (i, j))


def _t5_bucket(rel, num_buckets):
    half = num_buckets // 2
    max_exact = half // 2
    ret = jnp.where(rel > 0, half, 0)
    n = jnp.abs(rel)
    nf = jnp.maximum(n, 1).astype(_F32)
    large = max_exact + (jnp.log(nf / max_exact) / math.log(_REL_MAX_DIST / max_exact)
                         * (half - max_exact)).astype(jnp.int32)
    large = jnp.minimum(large, half - 1)
    return ret + jnp.where(n < max_exact, n, large)


def _band_body(rb_ref, o_ref, *, tq, tk, m_lo, num_buckets):
    h = pl.program_id(0)
    mi = pl.program_id(1)
    last = pl.num_programs(1) - 1

    @pl.when(mi == 0)
    def _():
        o_ref[...] = jnp.full((tq, tk), rb_ref[h, num_buckets // 2 - 1] * _LOG2E, _F32)

    @pl.when(mi == last)
    def _():
        o_ref[...] = jnp.full((tq, tk), rb_ref[h, num_buckets - 1] * _LOG2E, _F32)

    @pl.when((mi > 0) & (mi < last))
    def _():
        row = lax.broadcasted_iota(jnp.int32, (tq, tk), 0)
        col = lax.broadcasted_iota(jnp.int32, (tq, tk), 1)
        bucket = _t5_bucket((mi + m_lo) * tq + col - row, num_buckets)
        out = jnp.zeros((tq, tk), _F32)
        for b in range(num_buckets):
            out = jnp.where(bucket == b, rb_ref[h, b], out)
        o_ref[...] = out * _LOG2E


def _band_range(tq, tk):
    m_lo = -((_REL_MAX_DIST - 1 + tk + tq - 1) // tq)
    m_hi = (_REL_MAX_DIST - 1 + tq + tq - 1) // tq
    return m_lo, m_hi


def _bias_band(rel_bias, tq, tk):
    nbk, h = rel_bias.shape
    m_lo, m_hi = _band_range(tq, tk)
    nb = m_hi - m_lo + 1
    return pl.pallas_call(
        functools.partial(_band_body, tq=tq, tk=tk, m_lo=m_lo, num_buckets=nbk),
        out_shape=jax.ShapeDtypeStruct((h, nb, tq, tk), _F32),
        grid=(h, nb),
        in_specs=[pl.BlockSpec(memory_space=pltpu.SMEM)],
        out_specs=pl.BlockSpec((None, None, tq, tk), lambda i, j: (i, j, 0, 0)),
        compiler_params=_params(("parallel", "parallel"), 24 * _nbytes((tq, tk), _F32) + (4 << 20)),
        name="bias_band",
    )(rel_bias.T)


def _lane_fold(x, op):
    return functools.reduce(op, [x[:, c:c + _LANES] for c in range(0, x.shape[1], _LANES)])


def _attn_body(lam_ref, q_ref, k_ref, v_ref, band_ref, g_ref, o_ref, s_ref, p_ref,
               *, tq, tk, d, m_lo, m_hi, lambda_init):
    i = pl.program_id(2)
    nk = k_ref.shape[0] // tk
    ratio = tk // tq

    band_idx = [jnp.clip(j * ratio - i, m_lo, m_hi) - m_lo for j in range(nk)]

    def scores(mp):
        cols = slice(mp * d, (mp + 1) * d)
        qm = q_ref[:, cols]
        for j in range(nk):
            keys = slice(j * tk, (j + 1) * tk)
            s_ref[mp, :, keys] = lax.dot_general(qm, k_ref[keys, cols], _NT, preferred_element_type=_F32)

    def softmax_rows(mp):
        sums = []
        for r in range(0, tq, _ROW_GROUP):
            rows = slice(r, r + _ROW_GROUP)
            part_max = None
            for j in range(nk):
                keys = slice(j * tk, (j + 1) * tk)
                folded = _lane_fold(s_ref[mp, rows, keys] + band_ref[band_idx[j], rows, :], jnp.maximum)
                part_max = folded if part_max is None else jnp.maximum(part_max, folded)
            row_max = jnp.max(part_max, axis=-1, keepdims=True)
            part_sum = jnp.zeros((_ROW_GROUP, _LANES), _F32)
            for j in range(nk):
                keys = slice(j * tk, (j + 1) * tk)
                p = jnp.exp2(s_ref[mp, rows, keys] + (band_ref[band_idx[j], rows, :] - row_max))
                part_sum = part_sum + _lane_fold(p, jnp.add)
                p_ref[mp, rows, keys] = p.astype(p_ref.dtype)
            sums.append(jnp.sum(part_sum, axis=-1, keepdims=True))
        return jnp.concatenate(sums, axis=0)

    scores(0)
    scores(1)
    heads_out = []
    for mp in range(2):
        row_sum = softmax_rows(mp)
        heads_out.append(_dot(p_ref[mp], v_ref[...]) / row_sum)

    lv = lam_ref[...]
    lam = (jnp.exp(jnp.sum(lv[0:1] * lv[1:2], axis=-1, keepdims=True))
           - jnp.exp(jnp.sum(lv[2:3] * lv[3:4], axis=-1, keepdims=True)) + lambda_init)
    o = heads_out[0] - lam * heads_out[1]
    o_ref[...] = (_rms(o, g_ref[...]) * (1.0 - lambda_init)).astype(o_ref.dtype)


def _diff_attention(qk, v_src, band, lam_vecs, subln_g, batch, seq, heads, d, lambda_init, tq, tk):
    m = qk.shape[0]
    nq = seq // tq
    m_lo, m_hi = _band_range(tq, tk)
    nb = m_hi - m_lo + 1
    w = 2 * d
    vmem = (4 * _nbytes((seq, w), _BF16) + 2 * _nbytes((nb, tq, tk), _F32) + _nbytes((2, tq, seq), _F32)
            + _nbytes((2, tq, seq), _BF16) + 8 * _nbytes((tq, tk), _F32) + (6 << 20))
    return pl.pallas_call(
        functools.partial(_attn_body, tq=tq, tk=tk, d=d, m_lo=m_lo, m_hi=m_hi, lambda_init=lambda_init),
        out_shape=jax.ShapeDtypeStruct((m, heads * w), _BF16),
        grid=(batch, heads, nq),
        in_specs=[
            pl.BlockSpec((4, d), lambda b, h, i: (0, 0)),
            pl.BlockSpec((tq, w), lambda b, h, i: (b * nq + i, h)),
            pl.BlockSpec((seq, w), lambda b, h, i: (b, heads + h)),
            pl.BlockSpec((seq, w), lambda b, h, i: (b, h)),
            pl.BlockSpec((None, nb, tq, tk), lambda b, h, i: (h, 0, 0, 0)),
            pl.BlockSpec((1, w), lambda b, h, i: (0, 0)),
        ],
        out_specs=pl.BlockSpec((tq, w), lambda b, h, i: (b * nq + i, h)),
        scratch_shapes=[pltpu.VMEM((2, tq, seq), _F32), pltpu.VMEM((2, tq, seq), _BF16)],
        compiler_params=_params(("parallel", "parallel", "arbitrary"), vmem),
        name="diff_attention",
    )(lam_vecs, qk, qk, v_src, band, subln_g.reshape(1, w))


def _split(x):
    hi = x.astype(_BF16)
    return hi, (x - hi.astype(_F32)).astype(_BF16)


def _split_dot(lhs, parts, dims=None):
    if dims is None:
        return sum(_dot(lhs, part) for part in parts)
    return sum(lax.dot_general(part, lhs, dims, preferred_element_type=_F32) for part in parts)


def _gla_body(*refs, tt, dk, dv, hpb, reverse, final, q_scale):
    if final:
        (q_ref, k_ref, lr_ref, v_ref, wg_ref, bg_ref, fwd_ref, r_ref, gn_ref, o_ref, s_ref) = refs
    else:
        (q_ref, k_ref, lr_ref, v_ref, wg_ref, bg_ref, o_ref, s_ref) = refs
    c_len = _GLA_CHUNK
    nc = tt // c_len

    @pl.when(pl.program_id(2) == 0)
    def _():
        s_ref[...] = jnp.zeros_like(s_ref)

    row = lax.broadcasted_iota(jnp.int32, (tt, tt), 0)
    col = lax.broadcasted_iota(jnp.int32, (tt, tt), 1)
    causal = ((row // c_len) == (col // c_len)) & ((col >= row) if reverse else (col <= row))
    causal_ones = causal.astype(_BF16)
    ones_cols = jnp.ones((tt, _LANES), _BF16)
    stems = lr_ref[...].astype(_BF16)
    order = list(reversed(range(nc))) if reverse else list(range(nc))
    edge = 0 if reverse else c_len - 1

    def chunk_rows(x, c):
        return x[c * c_len:(c + 1) * c_len]

    def per_chunk(fn):
        return jnp.concatenate([fn(c) for c in range(nc)], axis=0)

    for hh in range(hpb):
        kc = slice(hh * dk, (hh + 1) * dk)
        vc = slice(hh * dv, (hh + 1) * dv)
        v = v_ref[:, vc]
        pre = _dot(stems, wg_ref[:, kc]) + bg_ref[:, kc]
        log_a = (jnp.minimum(pre, 0.0) - jnp.log(1.0 + jnp.exp(-jnp.abs(pre)))) * (_LOG2E / _GLA_TAU)
        log_a_parts = _split(log_a)
        b = _split_dot(causal_ones, log_a_parts)
        total = [chunk_rows(b, c)[edge:edge + 1] for c in range(nc)]
        start, run = {}, jnp.zeros((1, dk), _F32)
        for c in order:
            start[c] = run
            run = run + total[c]

        q_own = q_ref[:, kc] * q_scale * jnp.exp2(b)
        k_own = k_ref[:, kc] * jnp.exp2(per_chunk(lambda c: total[c] - chunk_rows(b, c)))
        q_dec = q_own.astype(_BF16)
        k_inv = (k_ref[:, kc] * jnp.exp2(-b)).astype(_BF16)
        att = jnp.where(causal, lax.dot_general(q_dec, k_inv, _NT, preferred_element_type=_F32), 0.0)

        def seen_keys(c):
            pos = order.index(c)
            return per_chunk(lambda c2: (chunk_rows(k_own, c2) * jnp.exp2(start[c] - start[c2] - total[c2])
                                         if order.index(c2) < pos else jnp.zeros((c_len, dk), _F32)))

        att = att + per_chunk(lambda c: (
            jnp.zeros((c_len, tt), _F32) if c == order[0] else
            lax.dot_general(chunk_rows(q_dec, c), seen_keys(c).astype(_BF16), _NT, preferred_element_type=_F32)))

        s = s_ref[hh]
        q_state = per_chunk(lambda c: chunk_rows(q_own, c) * jnp.exp2(start[c])).astype(_BF16)
        k_state = per_chunk(lambda c: chunk_rows(k_own, c) * jnp.exp2(run - start[c] - total[c])).astype(_BF16)
        o = _dot(att.astype(_BF16), v) + _dot(q_state, s.astype(_BF16))
        tile_decay = jnp.exp2(_split_dot(ones_cols, log_a_parts, _TN))[:, 0:1]
        s_ref[hh] = tile_decay * s + lax.dot_general(k_state, v, _TN, preferred_element_type=_F32)
        if final:
            o = _rms(o + fwd_ref[:, vc], gn_ref[...])
            r = r_ref[:, vc].astype(_F32)
            o_ref[:, vc] = (o * (r * (0.5 * jnp.tanh(0.5 * r) + 0.5))).astype(o_ref.dtype)
        else:
            o_ref[:, vc] = o


def _gla(qk, stems, vgr, w_gate, b_gate, batch, seq, heads, dk, dv, v_col0, reverse,
         fwd=None, r_col0=None, norm_g=None):
    m = qk.shape[0]
    tt = _tile(seq, 256, _GLA_CHUNK)
    nt = seq // tt
    hpb = _tile(heads, 4, 1)
    ng = heads // hpb
    final = fwd is not None
    assert v_col0 % hpb == 0 and (r_col0 is None or r_col0 % hpb == 0)

    def rows(b, t):
        return b * nt + (nt - 1 - t if reverse else t)

    in_specs = [
        pl.BlockSpec((tt, hpb * dk), lambda b, g, t: (rows(b, t), g)),
        pl.BlockSpec((tt, hpb * dk), lambda b, g, t: (rows(b, t), ng + g)),
        pl.BlockSpec((tt, dk), lambda b, g, t: (rows(b, t), 0)),
        pl.BlockSpec((tt, hpb * dv), lambda b, g, t: (rows(b, t), v_col0 // hpb + g)),
        pl.BlockSpec((dk, hpb * dk), lambda b, g, t: (0, g)),
        pl.BlockSpec((1, hpb * dk), lambda b, g, t: (0, g)),
    ]
    args = [qk, qk, stems, vgr, w_gate, b_gate]
    if final:
        in_specs += [
            pl.BlockSpec((tt, hpb * dv), lambda b, g, t: (rows(b, t), g)),
            pl.BlockSpec((tt, hpb * dv), lambda b, g, t: (rows(b, t), r_col0 // hpb + g)),
            pl.BlockSpec((1, dv), lambda b, g, t: (0, 0)),
        ]
        args += [fwd, vgr, norm_g.reshape(1, dv)]
    return pl.pallas_call(
        functools.partial(_gla_body, tt=tt, dk=dk, dv=dv, hpb=hpb, reverse=reverse, final=final,
                          q_scale=dk ** -0.5),
        out_shape=jax.ShapeDtypeStruct((m, heads * dv), _BF16 if final else _F32),
        grid=(batch, ng, nt),
        in_specs=in_specs,
        out_specs=pl.BlockSpec((tt, hpb * dv), lambda b, g, t: (rows(b, t), g)),
        scratch_shapes=[pltpu.VMEM((hpb, dk, dv), _F32)],
        compiler_params=_params(("parallel", "parallel", "arbitrary"), 48 << 20),
        name="gla_bwd" if reverse else "gla_fwd",
    )(*args)


def _merge_body(oa_ref, ob_ref, pa_ref, pb_ref, ga_ref, gb_ref, o_ref):
    a = _dot(oa_ref[...], pa_ref[...])
    b = _dot(ob_ref[...], pb_ref[...])
    o_ref[...] = (ga_ref[...].astype(_F32) * a + gb_ref[...].astype(_F32) * b).astype(o_ref.dtype)


def _merge(o_a, o_b, p_a, p_b, gates):
    m, ka = o_a.shape
    kb = o_b.shape[1]
    n = p_a.shape[1]
    tm = _tile(m, 1024, 16)
    tn = _tile(n, 1024, _LANES)
    nj = n // tn
    vmem = (2 * _nbytes((tm, ka + kb), _BF16) + 2 * _nbytes((ka + kb, tn), _BF16)
            + 4 * _nbytes((tm, tn), _BF16) + 6 * _nbytes((tm, tn), _F32) + (4 << 20))
    return pl.pallas_call(
        _merge_body,
        out_shape=jax.ShapeDtypeStruct((m, n), _BF16),
        grid=(m // tm, nj),
        in_specs=[
            pl.BlockSpec((tm, ka), lambda i, j: (i, 0)),
            pl.BlockSpec((tm, kb), lambda i, j: (i, 0)),
            pl.BlockSpec((ka, tn), lambda i, j: (0, j)),
            pl.BlockSpec((kb, tn), lambda i, j: (0, j)),
            pl.BlockSpec((tm, tn), lambda i, j: (i, j)),
            pl.BlockSpec((tm, tn), lambda i, j: (i, nj + j)),
        ],
        out_specs=pl.BlockSpec((tm, tn), lambda i, j: (i, j)),
        compiler_params=_params(("parallel", "arbitrary"), vmem),
        name="branch_merge",
    )(o_a, o_b, p_a, p_b, gates, gates)


def _gelu_tanh(x):
    return x * (0.5 * (1.0 + jnp.tanh(math.sqrt(2.0 / math.pi) * (x + 0.044715 * (x * x * x)))))


def _ffn_up_weight_body(w_ref, o_ref, *, nblk):
    t = pl.program_id(0)
    q = 2 * (t // 4) + t % 2
    o_ref[...] = jnp.where(q < nblk, w_ref[...], 0.0).astype(o_ref.dtype)


def _ffn_up_weight(w_up, f):
    d = w_up.shape[0]
    blk = _FFN_TILE // 2
    assert f % blk == 0
    nblk = f // blk
    n_out = 4 * (-(-f // _FFN_TILE))

    def src(t):
        q = 2 * (t // 4) + t % 2
        return 0, jnp.minimum(jnp.where(t % 4 < 2, q, nblk + q), 2 * nblk - 1)

    return pl.pallas_call(
        functools.partial(_ffn_up_weight_body, nblk=nblk),
        out_shape=jax.ShapeDtypeStruct((d, n_out * blk), _BF16),
        grid=(n_out,),
        in_specs=[pl.BlockSpec((d, blk), src)],
        out_specs=pl.BlockSpec((d, blk), lambda t: (0, t)),
        compiler_params=_params(("parallel",), 6 * _nbytes((d, blk), _F32) + (4 << 20)),
        name="ffn_up_weight",
    )(w_up)


def _ffn_up_body(h_ref, wa_ref, wb_ref, cw_ref, cb_ref, o_ref, *, tm, last_width):
    def gated(width):
        cols = slice(0, width)
        a = _dot(h_ref[...], wa_ref[:, cols])
        g = _dot(h_ref[0:tm, :], wb_ref[:, cols])
        rows = a.shape[0]
        prev = pltpu.roll(a, 1, 0)[0:tm]
        nxt = pltpu.roll(a, rows - 1, 0)[0:tm]
        conv = prev * cw_ref[0:1, cols] + a[0:tm] * cw_ref[1:2, cols] + nxt * cw_ref[2:3, cols] + cb_ref[:, cols]
        o_ref[:, cols] = (_gelu_tanh(conv) * g).astype(o_ref.dtype)

    tf = o_ref.shape[1]
    if last_width == tf:
        gated(tf)
    else:
        is_last = pl.program_id(1) == pl.num_programs(1) - 1
        pl.when(jnp.logical_not(is_last))(functools.partial(gated, tf))
        pl.when(is_last)(functools.partial(gated, last_width))


def _ffn_up(h_tiles, w_ag, conv_w, conv_b, tm, f):
    nt, rows, d = h_tiles.shape
    tf = _FFN_TILE
    vmem = (2 * _nbytes((rows, d), _BF16) + 4 * _nbytes((d, tf), _BF16) + 2 * _nbytes((tm, tf), _BF16)
            + 8 * _nbytes((rows, tf), _F32) + (4 << 20))
    return pl.pallas_call(
        functools.partial(_ffn_up_body, tm=tm, last_width=f % tf or tf),
        out_shape=jax.ShapeDtypeStruct((nt * tm, f), _BF16),
        grid=(nt, w_ag.shape[1] // (2 * tf)),
        in_specs=[
            pl.BlockSpec((None, rows, d), lambda i, j: (i, 0, 0)),
            pl.BlockSpec((d, tf), lambda i, j: (0, 2 * j)),
            pl.BlockSpec((d, tf), lambda i, j: (0, 2 * j + 1)),
            pl.BlockSpec((3, tf), lambda i, j: (0, j)),
            pl.BlockSpec((1, tf), lambda i, j: (0, j)),
        ],
        out_specs=pl.BlockSpec((tm, tf), lambda i, j: (i, j)),
        compiler_params=_params(("parallel", "arbitrary"), vmem),
        name="ffn_up",
    )(h_tiles, w_ag, w_ag, conv_w, conv_b)


def _pad_cols(w, n):
    return jnp.pad(w, ((0, 0), (0, n - w.shape[1])))


def _prepare_weights(lambda_init, rel_bias, g_mix, w_in, q_norm_g, k_norm_g, lambda_q1, lambda_k1, lambda_q2,
                     lambda_k2, da_subln_g, w_gate_fwd, b_gate_fwd, w_gate_bwd, b_gate_bwd, gla_norm_g,
                     w_branch_a, w_branch_b, w_out, g_ffn, w_up, conv_w, conv_b, w_down):
    heads = rel_bias.shape[1]
    d = q_norm_g.shape[-1]
    da = heads * 2 * d
    rank, gla_k = w_gate_fwd.shape
    dv = gla_norm_g.shape[-1]
    gla_v = w_branch_b.shape[0]
    gh = gla_v // dv
    dk = gla_k // gh
    f = conv_b.shape[-1]
    d_model = w_in.shape[0]
    widths = [da, da, da, gla_k, gla_k, gla_v, gla_v, rank, rank, d_model, d_model]
    starts = [0]
    for wd in widths:
        starts.append(starts[-1] + wd)
    sec = lambda s: (starts[s], widths[s])

    w_stems = _pad_cols(w_in[:, starts[7]:starts[9]].astype(_BF16), dk)
    w_gates = w_in[:, starts[9]:].astype(_BF16)
    qk_gain = jnp.concatenate([jnp.tile(q_norm_g * (d ** -0.5 * _LOG2E), 2 * heads), jnp.tile(k_norm_g, 2 * heads)])

    def gate_weight(w, first_row):
        return jnp.zeros((dk, gla_k), _BF16).at[first_row:first_row + rank].set(w.astype(_BF16))

    fp = -(-f // _FFN_TILE) * _FFN_TILE
    return dict(
        heads=heads, d=d, gh=gh, dk=dk, dv=dv, lambda_init=lambda_init,
        rel_bias=rel_bias, g_mix=g_mix, g_ffn=g_ffn,
        w_in=w_in.astype(_BF16), sec_qk=(sec(0), sec(1)), sec_vgr=(sec(2), sec(5), sec(6)),
        sec_gla_qk=(sec(3), sec(4)),
        w_stems=w_stems, w_gates=w_gates, qk_gain=qk_gain.reshape(1, -1),
        lam_vecs=jnp.stack([lambda_q1, lambda_k1, lambda_q2, lambda_k2]),
        da_subln_g=da_subln_g, gla_norm_g=gla_norm_g,
        wg_fwd=gate_weight(w_gate_fwd, 0), wg_bwd=gate_weight(w_gate_bwd, rank),
        bg_fwd=b_gate_fwd.reshape(1, -1), bg_bwd=b_gate_bwd.reshape(1, -1),
        p_a=w_branch_a.astype(_BF16), p_b=w_branch_b.astype(_BF16), w_out=w_out.astype(_BF16),
        f=f, w_up=_ffn_up_weight(w_up, f),
        conv_w=_pad_cols(conv_w, fp), conv_b=_pad_cols(conv_b.reshape(1, f), fp),
        w_down=w_down.astype(_BF16),
    )


def _encoder_layer(x3, p, band, tq, tk):
    batch, seq, d_model = x3.shape
    x = x3.reshape(batch * seq, d_model)
    heads, d, gh, dk, dv = p["heads"], p["d"], p["gh"], p["dk"], p["dv"]

    h = _rmsnorm(x, p["g_mix"])
    qk = _project(functools.partial(_proj_groupnorm_body, group=d), h, p["w_in"], _BF16, p["sec_qk"],
                  extra=(p["qk_gain"],), extra_specs=(_row_vec_spec,), name="proj_qk")
    vgr = _project(_proj_cast_body, h, p["w_in"], _BF16, p["sec_vgr"], name="proj_vgr")
    gla_qk = _project(_proj_cast_body, h, p["w_in"], _F32, p["sec_gla_qk"], name="proj_gla_qk")
    stems = _project(_proj_cast_body, h, p["w_stems"], _F32, name="proj_stems")
    gates = _project(_proj_sigmoid_body, h, p["w_gates"], _BF16, name="proj_gates")

    o_a = _diff_attention(qk, vgr, band, p["lam_vecs"], p["da_subln_g"], batch, seq, heads, d,
                          p["lambda_init"], tq, tk)

    v_col0 = (heads * 2 * d) // dv
    r_col0 = v_col0 + gh
    fwd = _gla(gla_qk, stems, vgr, p["wg_fwd"], p["bg_fwd"], batch, seq, gh, dk, dv, v_col0, reverse=False)
    o_b = _gla(gla_qk, stems, vgr, p["wg_bwd"], p["bg_bwd"], batch, seq, gh, dk, dv, v_col0, reverse=True,
               fwd=fwd, r_col0=r_col0, norm_g=p["gla_norm_g"])

    merged = _merge(o_a, o_b, p["p_a"], p["p_b"], gates)
    x1 = _project(_proj_residual_body, merged, p["w_out"], _F32, extra=(x,), extra_specs=(_tile_spec,),
                  name="proj_out")

    tm = _tile(seq, 1024, 2 * _HALO)
    h2 = _rmsnorm_halo(x1, p["g_ffn"], seq, tm)
    act = _ffn_up(h2, p["w_up"], p["conv_w"], p["conv_b"], tm, p["f"])
    y = _project(_proj_residual_body, act, p["w_down"], _F32, extra=(x1,), extra_specs=(_tile_spec,),
                 tile=_DOWN_TILE, name="ffn_down")
    return y.reshape(batch, seq, d_model)


def kernel(x_prompt, x_sample, rel_bias, g_mix, w_in, q_norm_g, k_norm_g, lambda_q1, lambda_k1, lambda_q2,
           lambda_k2, da_subln_g, w_gate_fwd, b_gate_fwd, w_gate_bwd, b_gate_bwd, gla_norm_g, w_branch_a,
           w_branch_b, w_out, g_ffn, w_up, conv_w, conv_b, w_down):
    layer_weights = (g_mix, w_in, q_norm_g, k_norm_g, lambda_q1, lambda_k1, lambda_q2, lambda_k2, da_subln_g,
                     w_gate_fwd, b_gate_fwd, w_gate_bwd, b_gate_bwd, gla_norm_g, w_branch_a, w_branch_b, w_out,
                     g_ffn, w_up, conv_w, conv_b, w_down)
    min_seq = min(x_prompt.shape[1], x_sample.shape[1])
    tq = _tile(min_seq, 512, _LANES)
    tk = _tile(min_seq, 512, tq)
    band = _bias_band(rel_bias, tq, tk)
    y_prompt, y_sample = x_prompt, x_sample
    for l in range(g_mix.shape[0]):
        lambda_init = 0.8 - 0.6 * math.exp(-0.3 * l)
        p = _prepare_weights(lambda_init, rel_bias, *(w[l] for w in layer_weights))
        y_prompt = _encoder_layer(y_prompt, p, band, tq, tk)
        y_sample = _encoder_layer(y_sample, p, band, tq, tk)
    return (y_prompt, y_sample)
```

```python
import functools
import math

import jax
import jax.numpy as jnp
from jax import lax
from jax.experimental import pallas as pl
from jax.experimental.pallas import tpu as pltpu

_EPS = 1e-6
_GLA_CHUNK = 64
_GLA_TAU = 16.0
_REL_MAX_DIST = 128
_LOG2E = math.log2(math.e)
_LANES = 128
_HALO = 16
_ROW_GROUP = 16
_FFN_TILE = 512
_DOWN_TILE = 512
_V7X_VMEM_BYTES = 64 * 1024 * 1024
_VMEM_BUDGET = _V7X_VMEM_BYTES - 8 * 1024 * 1024

_F32 = jnp.float32
_BF16 = jnp.bfloat16
_NT = (((1,), (1,)), ((), ()))
_TN = (((0,), (0,)), ((), ()))


def _params(semantics, vmem_bytes):
    return pltpu.CompilerParams(dimension_semantics=semantics,
                                vmem_limit_bytes=int(min(vmem_bytes, _VMEM_BUDGET)))


def _tile(n, pref, align):
    t = min(pref, n)
    t -= t % align
    while t > align and n % t:
        t -= align
    assert t >= align and n % t == 0, (n, pref, align)
    return t


def _nbytes(shape, dtype):
    return math.prod(shape) * jnp.dtype(dtype).itemsize


def _dot(a, b):
    return jnp.dot(a, b, preferred_element_type=_F32)


def _rms(x, g):
    ms = jnp.mean(x * x, axis=-1, keepdims=True)
    return x * lax.rsqrt(ms + _EPS) * g


def _rmsnorm_body(x_ref, g_ref, o_ref):
    o_ref[...] = _rms(x_ref[...], g_ref[...]).astype(o_ref.dtype)


def _rmsnorm(x, g):
    m, d = x.shape
    tm = _tile(m, 256, 8)
    return pl.pallas_call(
        _rmsnorm_body,
        out_shape=jax.ShapeDtypeStruct((m, d), _BF16),
        grid=(m // tm,),
        in_specs=[pl.BlockSpec((tm, d), lambda i: (i, 0)), pl.BlockSpec((1, d), lambda i: (0, 0))],
        out_specs=pl.BlockSpec((tm, d), lambda i: (i, 0)),
        compiler_params=_params(("parallel",), 6 * _nbytes((tm, d), _F32) + (8 << 20)),
        name="rmsnorm",
    )(x, g.reshape(1, d))


def _proj_out_body(a_ref, b_ref, r_ref, x_ref, xt_ref, ssq_ref):
    tm = x_ref.shape[0]
    x = r_ref[...] + _dot(a_ref[...], b_ref[...])
    x_ref[...] = x
    xt_ref[0:tm, :] = x.astype(xt_ref.dtype)
    xt_ref[tm:, :] = jnp.zeros_like(xt_ref[tm:, :])
    part = jnp.sum(x * x, axis=-1, keepdims=True)

    @pl.when(pl.program_id(1) == 0)
    def _():
        ssq_ref[0:tm, :] = part
        ssq_ref[tm:, :] = jnp.zeros_like(ssq_ref[tm:, :])

    @pl.when(pl.program_id(1) > 0)
    def _():
        ssq_ref[0:tm, :] += part


def _proj_out(a, b, resid, tm):
    m, k = a.shape
    n = b.shape[1]
    tn = _tile(n, 512, _LANES)
    nt = m // tm
    rows = tm + 2 * _HALO
    vmem = (2 * _nbytes((tm, k), _BF16) + 2 * _nbytes((k, tn), _BF16) + 8 * _nbytes((tm, tn), _F32) + (6 << 20))
    return pl.pallas_call(
        _proj_out_body,
        out_shape=[jax.ShapeDtypeStruct((m, n), _F32), jax.ShapeDtypeStruct((nt, rows, n), _BF16),
                   jax.ShapeDtypeStruct((nt, rows, 1), _F32)],
        grid=(nt, n // tn),
        in_specs=[pl.BlockSpec((tm, k), lambda i, j: (i, 0)),
                  pl.BlockSpec((k, tn), lambda i, j: (0, j)),
                  pl.BlockSpec((tm, tn), lambda i, j: (i, j))],
        out_specs=[pl.BlockSpec((tm, tn), lambda i, j: (i, j)),
                   pl.BlockSpec((None, rows, tn), lambda i, j: (i, 0, j)),
                   pl.BlockSpec((None, rows, 1), lambda i, j: (i, 0, 0))],
        compiler_params=_params(("parallel", "arbitrary"), vmem),
        name="proj_out",
    )(a, b, resid)


def _conv_halo_body(xt_hbm, ssq_hbm, xt_out, ssq_out, x_buf, s_buf, sems, *, tm, tiles_per_seq):
    i = pl.program_id(0)
    nt = pl.num_programs(0)
    nxt = jnp.minimum(i + 1, nt - 1)
    prv = jnp.maximum(i - 1, 0)
    copies = [
        pltpu.make_async_copy(xt_hbm.at[nxt, pl.ds(0, _HALO)], x_buf.at[0], sems.at[0]),
        pltpu.make_async_copy(xt_hbm.at[prv, pl.ds(tm - _HALO, _HALO)], x_buf.at[1], sems.at[1]),
        pltpu.make_async_copy(ssq_hbm.at[nxt, pl.ds(0, _HALO)], s_buf.at[0], sems.at[2]),
        pltpu.make_async_copy(ssq_hbm.at[prv, pl.ds(tm - _HALO, _HALO)], s_buf.at[1], sems.at[3]),
    ]
    for copy in copies:
        copy.start()
    for copy in copies:
        copy.wait()
    pos = i % tiles_per_seq
    has_next = pos < tiles_per_seq - 1
    has_prev = pos > 0
    xt_out[0:_HALO, :] = jnp.where(has_next, x_buf[0], jnp.zeros_like(x_buf[0]))
    xt_out[_HALO:2 * _HALO, :] = jnp.where(has_prev, x_buf[1], jnp.zeros_like(x_buf[1]))
    ssq_out[0:_HALO, :] = jnp.where(has_next, s_buf[0], 0.0)
    ssq_out[_HALO:2 * _HALO, :] = jnp.where(has_prev, s_buf[1], 0.0)


def _conv_halo(x_tiles, ssq_tiles, seq, tm):
    nt, rows, d = x_tiles.shape
    tail = tm // (2 * _HALO)
    return pl.pallas_call(
        functools.partial(_conv_halo_body, tm=tm, tiles_per_seq=seq // tm),
        out_shape=[jax.ShapeDtypeStruct(x_tiles.shape, x_tiles.dtype),
                   jax.ShapeDtypeStruct(ssq_tiles.shape, ssq_tiles.dtype)],
        grid=(nt,),
        in_specs=[pl.BlockSpec(memory_space=pl.ANY), pl.BlockSpec(memory_space=pl.ANY)],
        out_specs=[pl.BlockSpec((None, 2 * _HALO, d), lambda i: (i, tail, 0)),
                   pl.BlockSpec((None, 2 * _HALO, 1), lambda i: (i, tail, 0))],
        scratch_shapes=[pltpu.VMEM((2, _HALO, d), x_tiles.dtype), pltpu.VMEM((2, _HALO, 1), ssq_tiles.dtype),
                        pltpu.SemaphoreType.DMA((4,))],
        input_output_aliases={0: 0, 1: 1},
        compiler_params=_params(("arbitrary",), 16 << 20),
        name="conv_halo",
    )(x_tiles, ssq_tiles)


def _proj_cast_body(a_ref, b_ref, o_ref):
    o_ref[...] = _dot(a_ref[...], b_ref[...]).astype(o_ref.dtype)


def _proj_sigmoid_body(a_ref, b_ref, o_ref):
    o_ref[...] = (0.5 * jnp.tanh(0.5 * _dot(a_ref[...], b_ref[...])) + 0.5).astype(o_ref.dtype)


def _proj_groupnorm_body(a_ref, b_ref, g_ref, o_ref, *, group):
    acc = _dot(a_ref[...], b_ref[...])
    for c in range(acc.shape[1] // group):
        cols = slice(c * group, (c + 1) * group)
        o_ref[:, cols] = _rms(acc[:, cols], g_ref[:, cols]).astype(o_ref.dtype)


def _proj_residual_body(a_ref, b_ref, r_ref, o_ref):
    o_ref[...] = r_ref[...] + _dot(a_ref[...], b_ref[...])


def _project(body, a, b, out_dtype, sections=None, extra=(), extra_specs=(), tile=1024, name="project"):
    m, k = a.shape
    sections = sections or ((0, b.shape[1]),)
    n = sum(width for _, width in sections)
    tm = _tile(m, tile, 16)
    tn = _tile(math.gcd(*(v for sec in sections for v in sec)), tile, _LANES)
    first_out, shifts, done = [], [], 0
    for start, width in sections:
        first_out.append(done // tn)
        shifts.append(start // tn - done // tn)
        done += width

    def src_block(j):
        shift = shifts[0]
        for first, s in zip(first_out[1:], shifts[1:]):
            shift = jnp.where(j >= first, s, shift)
        return j + shift

    vmem = (2 * _nbytes((tm, k), a.dtype) + 2 * _nbytes((k, tn), b.dtype)
            + 6 * _nbytes((tm, tn), _F32) + (4 << 20))
    return pl.pallas_call(
        body,
        out_shape=jax.ShapeDtypeStruct((m, n), out_dtype),
        grid=(m // tm, n // tn),
        in_specs=[pl.BlockSpec((tm, k), lambda i, j: (i, 0)),
                  pl.BlockSpec((k, tn), lambda i, j: (0, src_block(j))),
                  *[spec(tm, tn) for spec in extra_specs]],
        out_specs=pl.BlockSpec((tm, tn), lambda i, j: (i, j)),
        compiler_params=_params(("parallel", "arbitrary"), vmem),
        name=name,
    )(a, b, *extra)


def _row_vec_spec(tm, tn):
    return pl.BlockSpec((1, tn), lambda i, j: (0, j))


def _tile_spec(tm, tn):
    return pl.BlockSpec((tm, tn), lambda i, j: (i, j))


def _t5_bucket(rel, num_buckets):
    half = num_buckets // 2
    max_exact = half // 2
    ret = jnp.where(rel > 0, half, 0)
    n = jnp.abs(rel)
    nf = jnp.maximum(n, 1).astype(_F32)
    large = max_exact + (jnp.log(nf / max_exact) / math.log(_REL_MAX_DIST / max_exact)
                         * (half - max_exact)).astype(jnp.int32)
    large = jnp.minimum(large, half - 1)
    return ret + jnp.where(n < max_exact, n, large)


def _band_body(rb_ref, o_ref, *, tq, tk, m_lo, num_buckets):
    h = pl.program_id(0)
    mi = pl.program_id(1)
    last = pl.num_programs(1) - 1

    @pl.when(mi == 0)
    def _():
        o_ref[...] = jnp.full((tq, tk), rb_ref[h, num_buckets // 2 - 1] * _LOG2E, _F32)

    @pl.when(mi == last)
    def _():
        o_ref[...] = jnp.full((tq, tk), rb_ref[h, num_buckets - 1] * _LOG2E, _F32)

    @pl.when((mi > 0) & (mi < last))
    def _():
        row = lax.broadcasted_iota(jnp.int32, (tq, tk), 0)
        col = lax.broadcasted_iota(jnp.int32, (tq, tk), 1)
        bucket = _t5_bucket((mi + m_lo) * tq + col - row, num_buckets)
        out = jnp.zeros((tq, tk), _F32)
        for b in range(num_buckets):
            out = jnp.where(bucket == b, rb_ref[h, b], out)
        o_ref[...] = out * _LOG2E


def _band_range(tq, tk):
    m_lo = -((_REL_MAX_DIST - 1 + tk + tq - 1) // tq)
    m_hi = (_REL_MAX_DIST - 1 + tq + tq - 1) // tq
    return m_lo, m_hi


def _bias_band(rel_bias, tq, tk):
    nbk, h = rel_bias.shape
    m_lo, m_hi = _band_range(tq, tk)
    nb = m_hi - m_lo + 1
    return pl.pallas_call(
        functools.partial(_band_body, tq=tq, tk=tk, m_lo=m_lo, num_buckets=nbk),
        out_shape=jax.ShapeDtypeStruct((h, nb, tq, tk), _F32),
        grid=(h, nb),
        in_specs=[pl.BlockSpec(memory_space=pltpu.SMEM)],
        out_specs=pl.BlockSpec((None, None, tq, tk), lambda i, j: (i, j, 0, 0)),
        compiler_params=_params(("parallel", "parallel"), 24 * _nbytes((tq, tk), _F32) + (4 << 20)),
        name="bias_band",
    )(rel_bias.T)


def _lane_fold(x, op):
    return functools.reduce(op, [x[:, c:c + _LANES] for c in range(0, x.shape[1], _LANES)])


def _attn_body(lam_ref, q_ref, k_ref, v_ref, band_ref, g_ref, o_ref, s_ref, p_ref,
               *, tq, tk, d, m_lo, m_hi, lambda_init):
    i = pl.program_id(2)
    nk = k_ref.shape[0] // tk
    ratio = tk // tq

    band_idx = [jnp.clip(j * ratio - i, m_lo, m_hi) - m_lo for j in range(nk)]

    def scores(mp):
        cols = slice(mp * d, (mp + 1) * d)
        qm = q_ref[:, cols]
        for j in range(nk):
            keys = slice(j * tk, (j + 1) * tk)
            s_ref[mp, :, keys] = lax.dot_general(qm, k_ref[keys, cols], _NT, preferred_element_type=_F32)

    def softmax_rows(mp):
        sums = []
        for r in range(0, tq, _ROW_GROUP):
            rows = slice(r, r + _ROW_GROUP)
            part_max = None
            for j in range(nk):
                keys = slice(j * tk, (j + 1) * tk)
                folded = _lane_fold(s_ref[mp, rows, keys] + band_ref[band_idx[j], rows, :], jnp.maximum)
                part_max = folded if part_max is None else jnp.maximum(part_max, folded)
            row_max = jnp.max(part_max, axis=-1, keepdims=True)
            part_sum = jnp.zeros((_ROW_GROUP, _LANES), _F32)
            for j in range(nk):
                keys = slice(j * tk, (j + 1) * tk)
                p = jnp.exp2(s_ref[mp, rows, keys] + (band_ref[band_idx[j], rows, :] - row_max))
                part_sum = part_sum + _lane_fold(p, jnp.add)
                p_ref[mp, rows, keys] = p.astype(p_ref.dtype)
            sums.append(jnp.sum(part_sum, axis=-1, keepdims=True))
        return jnp.concatenate(sums, axis=0)

    scores(0)
    scores(1)
    heads_out = []
    for mp in range(2):
        row_sum = softmax_rows(mp)
        heads_out.append(_dot(p_ref[mp], v_ref[...]) / row_sum)

    lv = lam_ref[...]
    lam = (jnp.exp(jnp.sum(lv[0:1] * lv[1:2], axis=-1, keepdims=True))
           - jnp.exp(jnp.sum(lv[2:3] * lv[3:4], axis=-1, keepdims=True)) + lambda_init)
    o = heads_out[0] - lam * heads_out[1]
    o_ref[...] = (_rms(o, g_ref[...]) * (1.0 - lambda_init)).astype(o_ref.dtype)


def _diff_attention(qk, v_src, band, lam_vecs, subln_g, batch, seq, heads, d, lambda_init, tq, tk):
    m = qk.shape[0]
    nq = seq // tq
    m_lo, m_hi = _band_range(tq, tk)
    nb = m_hi - m_lo + 1
    w = 2 * d
    vmem = (4 * _nbytes((seq, w), _BF16) + 2 * _nbytes((nb, tq, tk), _F32) + _nbytes((2, tq, seq), _F32)
            + _nbytes((2, tq, seq), _BF16) + 8 * _nbytes((tq, tk), _F32) + (6 << 20))
    return pl.pallas_call(
        functools.partial(_attn_body, tq=tq, tk=tk, d=d, m_lo=m_lo, m_hi=m_hi, lambda_init=lambda_init),
        out_shape=jax.ShapeDtypeStruct((m, heads * w), _BF16),
        grid=(batch, heads, nq),
        in_specs=[
            pl.BlockSpec((4, d), lambda b, h, i: (0, 0)),
            pl.BlockSpec((tq, w), lambda b, h, i: (b * nq + i, h)),
            pl.BlockSpec((seq, w), lambda b, h, i: (b, heads + h)),
            pl.BlockSpec((seq, w), lambda b, h, i: (b, h)),
            pl.BlockSpec((None, nb, tq, tk), lambda b, h, i: (h, 0, 0, 0)),
            pl.BlockSpec((1, w), lambda b, h, i: (0, 0)),
        ],
        out_specs=pl.BlockSpec((tq, w), lambda b, h, i: (b * nq + i, h)),
        scratch_shapes=[pltpu.VMEM((2, tq, seq), _F32), pltpu.VMEM((2, tq, seq), _BF16)],
        compiler_params=_params(("parallel", "parallel", "arbitrary"), vmem),
        name="diff_attention",
    )(lam_vecs, qk, qk, v_src, band, subln_g.reshape(1, w))


def _split(x):
    hi = x.astype(_BF16)
    return hi, (x - hi.astype(_F32)).astype(_BF16)


def _split_dot(lhs, parts, dims=None):
    if dims is None:
        return sum(_dot(lhs, part) for part in parts)
    return sum(lax.dot_general(part, lhs, dims, preferred_element_type=_F32) for part in parts)


def _gla_body(*refs, tt, dk, dv, hpb, reverse, final, q_scale):
    if final:
        (q_ref, k_ref, lr_ref, v_ref, wg_ref, bg_ref, fwd_ref, r_ref, gn_ref, o_ref, s_ref) = refs
    else:
        (q_ref, k_ref, lr_ref, v_ref, wg_ref, bg_ref, o_ref, s_ref) = refs
    c_len = _GLA_CHUNK
    nc = tt // c_len

    @pl.when(pl.program_id(2) == 0)
    def _():
        s_ref[...] = jnp.zeros_like(s_ref)

    row = lax.broadcasted_iota(jnp.int32, (tt, tt), 0)
    col = lax.broadcasted_iota(jnp.int32, (tt, tt), 1)
    causal = ((row // c_len) == (col // c_len)) & ((col >= row) if reverse else (col <= row))
    causal_ones = causal.astype(_BF16)
    ones_cols = jnp.ones((tt, _LANES), _BF16)
    stems = lr_ref[...].astype(_BF16)
    order = list(reversed(range(nc))) if reverse else list(range(nc))
    edge = 0 if reverse else c_len - 1

    def chunk_rows(x, c):
        return x[c * c_len:(c + 1) * c_len]

    def per_chunk(fn):
        return jnp.concatenate([fn(c) for c in range(nc)], axis=0)

    for hh in range(hpb):
        kc = slice(hh * dk, (hh + 1) * dk)
        vc = slice(hh * dv, (hh + 1) * dv)
        v = v_ref[:, vc]
        pre = _dot(stems, wg_ref[:, kc]) + bg_ref[:, kc]
        log_a = (jnp.minimum(pre, 0.0) - jnp.log(1.0 + jnp.exp(-jnp.abs(pre)))) * (_LOG2E / _GLA_TAU)
        log_a_parts = _split(log_a)
        b = _split_dot(causal_ones, log_a_parts)
        total = [chunk_rows(b, c)[edge:edge + 1] for c in range(nc)]
        start, run = {}, jnp.zeros((1, dk), _F32)
        for c in order:
            start[c] = run
            run = run + total[c]

        q_own = q_ref[:, kc] * q_scale * jnp.exp2(b)
        k_own = k_ref[:, kc] * jnp.exp2(per_chunk(lambda c: total[c] - chunk_rows(b, c)))
        q_dec = q_own.astype(_BF16)
        k_inv = (k_ref[:, kc] * jnp.exp2(-b)).astype(_BF16)
        att = jnp.where(causal, lax.dot_general(q_dec, k_inv, _NT, preferred_element_type=_F32), 0.0)

        def seen_keys(c):
            pos = order.index(c)
            return per_chunk(lambda c2: (chunk_rows(k_own, c2) * jnp.exp2(start[c] - start[c2] - total[c2])
                                         if order.index(c2) < pos else jnp.zeros((c_len, dk), _F32)))

        att = att + per_chunk(lambda c: (
            jnp.zeros((c_len, tt), _F32) if c == order[0] else
            lax.dot_general(chunk_rows(q_dec, c), seen_keys(c).astype(_BF16), _NT, preferred_element_type=_F32)))

        s = s_ref[hh]
        q_state = per_chunk(lambda c: chunk_rows(q_own, c) * jnp.exp2(start[c])).astype(_BF16)
        k_state = per_chunk(lambda c: chunk_rows(k_own, c) * jnp.exp2(run - start[c] - total[c])).astype(_BF16)
        o = _dot(att.astype(_BF16), v) + _dot(q_state, s.astype(_BF16))
        tile_decay = jnp.exp2(_split_dot(ones_cols, log_a_parts, _TN))[:, 0:1]
        s_ref[hh] = tile_decay * s + lax.dot_general(k_state, v, _TN, preferred_element_type=_F32)
        if final:
            o = _rms(o + fwd_ref[:, vc], gn_ref[...])
            r = r_ref[:, vc].astype(_F32)
            o_ref[:, vc] = (o * (r * (0.5 * jnp.tanh(0.5 * r) + 0.5))).astype(o_ref.dtype)
        else:
            o_ref[:, vc] = o


def _gla(qk, stems, vgr, w_gate, b_gate, batch, seq, heads, dk, dv, v_col0, reverse,
         fwd=None, r_col0=None, norm_g=None):
    m = qk.shape[0]
    tt = _tile(seq, 256, _GLA_CHUNK)
    nt = seq // tt
    hpb = _tile(heads, 4, 1)
    ng = heads // hpb
    final = fwd is not None
    assert v_col0 % hpb == 0 and (r_col0 is None or r_col0 % hpb == 0)

    def rows(b, t):
        return b * nt + (nt - 1 - t if reverse else t)

    in_specs = [
        pl.BlockSpec((tt, hpb * dk), lambda b, g, t: (rows(b, t), g)),
        pl.BlockSpec((tt, hpb * dk), lambda b, g, t: (rows(b, t), ng + g)),
        pl.BlockSpec((tt, dk), lambda b, g, t: (rows(b, t), 0)),
        pl.BlockSpec((tt, hpb * dv), lambda b, g, t: (rows(b, t), v_col0 // hpb + g)),
        pl.BlockSpec((dk, hpb * dk), lambda b, g, t: (0, g)),
        pl.BlockSpec((1, hpb * dk), lambda b, g, t: (0, g)),
    ]
    args = [qk, qk, stems, vgr, w_gate, b_gate]
    if final:
        in_specs += [
            pl.BlockSpec((tt, hpb * dv), lambda b, g, t: (rows(b, t), g)),
            pl.BlockSpec((tt, hpb * dv), lambda b, g, t: (rows(b, t), r_col0 // hpb + g)),
            pl.BlockSpec((1, dv), lambda b, g, t: (0, 0)),
        ]
        args += [fwd, vgr, norm_g.reshape(1, dv)]
    return pl.pallas_call(
        functools.partial(_gla_body, tt=tt, dk=dk, dv=dv, hpb=hpb, reverse=reverse, final=final,
                          q_scale=dk ** -0.5),
        out_shape=jax.ShapeDtypeStruct((m, heads * dv), _BF16 if final else _F32),
        grid=(batch, ng, nt),
        in_specs=in_specs,
        out_specs=pl.BlockSpec((tt, hpb * dv), lambda b, g, t: (rows(b, t), g)),
        scratch_shapes=[pltpu.VMEM((hpb, dk, dv), _F32)],
        compiler_params=_params(("parallel", "parallel", "arbitrary"), 48 << 20),
        name="gla_bwd" if reverse else "gla_fwd",
    )(*args)


def _merge_body(oa_ref, ob_ref, pa_ref, pb_ref, ga_ref, gb_ref, o_ref):
    a = _dot(oa_ref[...], pa_ref[...])
    b = _dot(ob_ref[...], pb_ref[...])
    o_ref[...] = (ga_ref[...].astype(_F32) * a + gb_ref[...].astype(_F32) * b).astype(o_ref.dtype)


def _merge(o_a, o_b, p_a, p_b, gates):
    m, ka = o_a.shape
    kb = o_b.shape[1]
    n = p_a.shape[1]
    tm = _tile(m, 1024, 16)
    tn = _tile(n, 1024, _LANES)
    nj = n // tn
    vmem = (2 * _nbytes((tm, ka + kb), _BF16) + 2 * _nbytes((ka + kb, tn), _BF16)
            + 4 * _nbytes((tm, tn), _BF16) + 6 * _nbytes((tm, tn), _F32) + (4 << 20))
    return pl.pallas_call(
        _merge_body,
        out_shape=jax.ShapeDtypeStruct((m, n), _BF16),
        grid=(m // tm, nj),
        in_specs=[
            pl.BlockSpec((tm, ka), lambda i, j: (i, 0)),
            pl.BlockSpec((tm, kb), lambda i, j: (i, 0)),
            pl.BlockSpec((ka, tn), lambda i, j: (0, j)),
            pl.BlockSpec((kb, tn), lambda i, j: (0, j)),
            pl.BlockSpec((tm, tn), lambda i, j: (i, j)),
            pl.BlockSpec((tm, tn), lambda i, j: (i, nj + j)),
        ],
        out_specs=pl.BlockSpec((tm, tn), lambda i, j: (i, j)),
        compiler_params=_params(("parallel", "arbitrary"), vmem),
        name="branch_merge",
    )(o_a, o_b, p_a, p_b, gates, gates)


def _gelu_tanh(x):
    return x * (0.5 * (1.0 + jnp.tanh(math.sqrt(2.0 / math.pi) * (x + 0.044715 * (x * x * x)))))


def _ffn_up_weight_body(w_ref, gain_ref, o_ref, *, nblk):
    t = pl.program_id(0)
    q = 2 * (t // 4) + t % 2
    o_ref[...] = jnp.where(q < nblk, w_ref[...] * gain_ref[...], 0.0).astype(o_ref.dtype)


def _ffn_up_weight(w_up, f, row_gain):
    d = w_up.shape[0]
    blk = _FFN_TILE // 2
    assert f % blk == 0
    nblk = f // blk
    n_out = 4 * (-(-f // _FFN_TILE))

    def src(t):
        q = 2 * (t // 4) + t % 2
        return 0, jnp.minimum(jnp.where(t % 4 < 2, q, nblk + q), 2 * nblk - 1)

    return pl.pallas_call(
        functools.partial(_ffn_up_weight_body, nblk=nblk),
        out_shape=jax.ShapeDtypeStruct((d, n_out * blk), _BF16),
        grid=(n_out,),
        in_specs=[pl.BlockSpec((d, blk), src), pl.BlockSpec((d, 1), lambda t: (0, 0))],
        out_specs=pl.BlockSpec((d, blk), lambda t: (0, t)),
        compiler_params=_params(("parallel",), 6 * _nbytes((d, blk), _F32) + _nbytes((d, _LANES), _F32) + (4 << 20)),
        name="ffn_up_weight",
    )(w_up, row_gain.reshape(d, 1))


def _ffn_up_body(h_ref, ssq_ref, wa_ref, wb_ref, cw_ref, cb_ref, o_ref, *, tm, last_width):
    inv_rms = lax.rsqrt(ssq_ref[...] * (1.0 / h_ref.shape[1]) + _EPS)

    def gated(width):
        cols = slice(0, width)
        a = _dot(h_ref[...], wa_ref[:, cols]) * inv_rms
        g = _dot(h_ref[0:tm, :], wb_ref[:, cols]) * inv_rms[0:tm]
        rows = a.shape[0]
        prev = pltpu.roll(a, 1, 0)[0:tm]
        nxt = pltpu.roll(a, rows - 1, 0)[0:tm]
        conv = prev * cw_ref[0:1, cols] + a[0:tm] * cw_ref[1:2, cols] + nxt * cw_ref[2:3, cols] + cb_ref[:, cols]
        o_ref[:, cols] = (_gelu_tanh(conv) * g).astype(o_ref.dtype)

    tf = o_ref.shape[1]
    if last_width == tf:
        gated(tf)
    else:
        is_last = pl.program_id(1) == pl.num_programs(1) - 1
        pl.when(jnp.logical_not(is_last))(functools.partial(gated, tf))
        pl.when(is_last)(functools.partial(gated, last_width))


def _ffn_up(h_tiles, ssq_tiles, w_ag, conv_w, conv_b, tm, f):
    nt, rows, d = h_tiles.shape
    tf = _FFN_TILE
    vmem = (2 * _nbytes((rows, d), _BF16) + 4 * _nbytes((d, tf), _BF16) + 2 * _nbytes((tm, tf), _BF16)
            + 8 * _nbytes((rows, tf), _F32) + (4 << 20))
    return pl.pallas_call(
        functools.partial(_ffn_up_body, tm=tm, last_width=f % tf or tf),
        out_shape=jax.ShapeDtypeStruct((nt * tm, f), _BF16),
        grid=(nt, w_ag.shape[1] // (2 * tf)),
        in_specs=[
            pl.BlockSpec((None, rows, d), lambda i, j: (i, 0, 0)),
            pl.BlockSpec((None, rows, 1), lambda i, j: (i, 0, 0)),
            pl.BlockSpec((d, tf), lambda i, j: (0, 2 * j)),
            pl.BlockSpec((d, tf), lambda i, j: (0, 2 * j + 1)),
            pl.BlockSpec((3, tf), lambda i, j: (0, j)),
            pl.BlockSpec((1, tf), lambda i, j: (0, j)),
        ],
        out_specs=pl.BlockSpec((tm, tf), lambda i, j: (i, j)),
        compiler_params=_params(("parallel", "arbitrary"), vmem),
        name="ffn_up",
    )(h_tiles, ssq_tiles, w_ag, w_ag, conv_w, conv_b)


def _pad_cols(w, n):
    return jnp.pad(w, ((0, 0), (0, n - w.shape[1])))


def _prepare_weights(lambda_init, rel_bias, g_mix, w_in, q_norm_g, k_norm_g, lambda_q1, lambda_k1, lambda_q2,
                     lambda_k2, da_subln_g, w_gate_fwd, b_gate_fwd, w_gate_bwd, b_gate_bwd, gla_norm_g,
                     w_branch_a, w_branch_b, w_out, g_ffn, w_up, conv_w, conv_b, w_down):
    heads = rel_bias.shape[1]
    d = q_norm_g.shape[-1]
    da = heads * 2 * d
    rank, gla_k = w_gate_fwd.shape
    dv = gla_norm_g.shape[-1]
    gla_v = w_branch_b.shape[0]
    gh = gla_v // dv
    dk = gla_k // gh
    f = conv_b.shape[-1]
    d_model = w_in.shape[0]
    widths = [da, da, da, gla_k, gla_k, gla_v, gla_v, rank, rank, d_model, d_model]
    starts = [0]
    for wd in widths:
        starts.append(starts[-1] + wd)
    sec = lambda s: (starts[s], widths[s])

    w_stems = _pad_cols(w_in[:, starts[7]:starts[9]].astype(_BF16), dk)
    w_gates = w_in[:, starts[9]:].astype(_BF16)
    qk_gain = jnp.concatenate([jnp.tile(q_norm_g * (d ** -0.5 * _LOG2E), 2 * heads), jnp.tile(k_norm_g, 2 * heads)])

    def gate_weight(w, first_row):
        return jnp.zeros((dk, gla_k), _BF16).at[first_row:first_row + rank].set(w.astype(_BF16))

    fp = -(-f // _FFN_TILE) * _FFN_TILE
    return dict(
        heads=heads, d=d, gh=gh, dk=dk, dv=dv, lambda_init=lambda_init,
        rel_bias=rel_bias, g_mix=g_mix, g_ffn=g_ffn,
        w_in=w_in.astype(_BF16), sec_qk=(sec(0), sec(1)), sec_vgr=(sec(2), sec(5), sec(6)),
        sec_gla_qk=(sec(3), sec(4)),
        w_stems=w_stems, w_gates=w_gates, qk_gain=qk_gain.reshape(1, -1),
        lam_vecs=jnp.stack([lambda_q1, lambda_k1, lambda_q2, lambda_k2]),
        da_subln_g=da_subln_g, gla_norm_g=gla_norm_g,
        wg_fwd=gate_weight(w_gate_fwd, 0), wg_bwd=gate_weight(w_gate_bwd, rank),
        bg_fwd=b_gate_fwd.reshape(1, -1), bg_bwd=b_gate_bwd.reshape(1, -1),
        p_a=w_branch_a.astype(_BF16), p_b=w_branch_b.astype(_BF16), w_out=w_out.astype(_BF16),
        f=f, w_up=_ffn_up_weight(w_up, f, g_ffn),
        conv_w=_pad_cols(conv_w, fp), conv_b=_pad_cols(conv_b.reshape(1, f), fp),
        w_down=w_down.astype(_BF16),
    )


def _encoder_layer(x3, p, band, tq, tk):
    batch, seq, d_model = x3.shape
    x = x3.reshape(batch * seq, d_model)
    heads, d, gh, dk, dv = p["heads"], p["d"], p["gh"], p["dk"], p["dv"]

    h = _rmsnorm(x, p["g_mix"])
    qk = _project(functools.partial(_proj_groupnorm_body, group=d), h, p["w_in"], _BF16, p["sec_qk"],
                  extra=(p["qk_gain"],), extra_specs=(_row_vec_spec,), name="proj_qk")
    vgr = _project(_proj_cast_body, h, p["w_in"], _BF16, p["sec_vgr"], name="proj_vgr")
    gla_qk = _project(_proj_cast_body, h, p["w_in"], _F32, p["sec_gla_qk"], name="proj_gla_qk")
    stems = _project(_proj_cast_body, h, p["w_stems"], _F32, name="proj_stems")
    gates = _project(_proj_sigmoid_body, h, p["w_gates"], _BF16, name="proj_gates")

    o_a = _diff_attention(qk, vgr, band, p["lam_vecs"], p["da_subln_g"], batch, seq, heads, d,
                          p["lambda_init"], tq, tk)

    v_col0 = (heads * 2 * d) // dv
    r_col0 = v_col0 + gh
    fwd = _gla(gla_qk, stems, vgr, p["wg_fwd"], p["bg_fwd"], batch, seq, gh, dk, dv, v_col0, reverse=False)
    o_b = _gla(gla_qk, stems, vgr, p["wg_bwd"], p["bg_bwd"], batch, seq, gh, dk, dv, v_col0, reverse=True,
               fwd=fwd, r_col0=r_col0, norm_g=p["gla_norm_g"])

    merged = _merge(o_a, o_b, p["p_a"], p["p_b"], gates)
    tm = _tile(seq, 1024, 2 * _HALO)
    x1, x1_tiles, ssq_tiles = _proj_out(merged, p["w_out"], x, tm)
    x1_tiles, ssq_tiles = _conv_halo(x1_tiles, ssq_tiles, seq, tm)
    act = _ffn_up(x1_tiles, ssq_tiles, p["w_up"], p["conv_w"], p["conv_b"], tm, p["f"])
    y = _project(_proj_residual_body, act, p["w_down"], _F32, extra=(x1,), extra_specs=(_tile_spec,),
                 tile=_DOWN_TILE, name="ffn_down")
    return y.reshape(batch, seq, d_model)


def kernel(x_prompt, x_sample, rel_bias, g_mix, w_in, q_norm_g, k_norm_g, lambda_q1, lambda_k1, lambda_q2,
           lambda_k2, da_subln_g, w_gate_fwd, b_gate_fwd, w_gate_bwd, b_gate_bwd, gla_norm_g, w_branch_a,
           w_branch_b, w_out, g_ffn, w_up, conv_w, conv_b, w_down):
    layer_weights = (g_mix, w_in, q_norm_g, k_norm_g, lambda_q1, lambda_k1, lambda_q2, lambda_k2, da_subln_g,
                     w_gate_fwd, b_gate_fwd, w_gate_bwd, b_gate_bwd, gla_norm_g, w_branch_a, w_branch_b, w_out,
                     g_ffn, w_up, conv_w, conv_b, w_down)
    min_seq = min(x_prompt.shape[1], x_sample.shape[1])
    tq = _tile(min_seq, 512, _LANES)
    tk = _tile(min_seq, 512, tq)
    band = _bias_band(rel_bias, tq, tk)
    y_prompt, y_sample = x_prompt, x_sample
    for l in range(g_mix.shape[0]):
        lambda_init = 0.8 - 0.6 * math.exp(-0.3 * l)
        p = _prepare_weights(lambda_init, rel_bias, *(w[l] for w in layer_weights))
        y_prompt = _encoder_layer(y_prompt, p, band, tq, tk)
        y_sample = _encoder_layer(y_sample, p, band, tq, tk)
    return (y_prompt, y_sample)
```

```python
import functools
import math

import jax
import jax.numpy as jnp
from jax import lax
from jax.experimental import pallas as pl
from jax.experimental.pallas import tpu as pltpu

_EPS = 1e-6
_GLA_CHUNK = 64
_GLA_TAU = 16.0
_REL_MAX_DIST = 128
_LOG2E = math.log2(math.e)
_LANES = 128
_HALO = 16
_ROW_GROUP = 16
_FFN_TILE = 512
_DOWN_TILE = 512
_V7X_VMEM_BYTES = 64 * 1024 * 1024
_VMEM_BUDGET = _V7X_VMEM_BYTES - 8 * 1024 * 1024

_F32 = jnp.float32
_BF16 = jnp.bfloat16
_NT = (((1,), (1,)), ((), ()))
_TN = (((0,), (0,)), ((), ()))


def _params(semantics, vmem_bytes):
    return pltpu.CompilerParams(dimension_semantics=semantics,
                                vmem_limit_bytes=int(min(vmem_bytes, _VMEM_BUDGET)))


def _tile(n, pref, align):
    t = min(pref, n)
    t -= t % align
    while t > align and n % t:
        t -= align
    assert t >= align and n % t == 0, (n, pref, align)
    return t


def _nbytes(shape, dtype):
    return math.prod(shape) * jnp.dtype(dtype).itemsize


def _dot(a, b):
    return jnp.dot(a, b, preferred_element_type=_F32)


def _rms(x, g):
    ms = jnp.mean(x * x, axis=-1, keepdims=True)
    return x * lax.rsqrt(ms + _EPS) * g


def _rmsnorm_body(x_ref, g_ref, o_ref):
    o_ref[...] = _rms(x_ref[...], g_ref[...]).astype(o_ref.dtype)


def _rmsnorm(x, g):
    m, d = x.shape
    tm = _tile(m, 512, 8)
    return pl.pallas_call(
        _rmsnorm_body,
        out_shape=jax.ShapeDtypeStruct((m, d), _BF16),
        grid=(m // tm,),
        in_specs=[pl.BlockSpec((tm, d), lambda i: (i, 0)), pl.BlockSpec((1, d), lambda i: (0, 0))],
        out_specs=pl.BlockSpec((tm, d), lambda i: (i, 0)),
        compiler_params=_params(("parallel",), 6 * _nbytes((tm, d), _F32) + (8 << 20)),
        name="rmsnorm",
    )(x, g.reshape(1, d))


def _proj_out_body(a_ref, b_ref, r_ref, x_ref, xt_ref, ssq_ref):
    tm = x_ref.shape[0]
    x = r_ref[...] + _dot(a_ref[...], b_ref[...])
    x_ref[...] = x
    xt_ref[0:tm, :] = x.astype(xt_ref.dtype)
    xt_ref[tm:, :] = jnp.zeros_like(xt_ref[tm:, :])
    part = jnp.sum(x * x, axis=-1, keepdims=True)

    @pl.when(pl.program_id(1) == 0)
    def _():
        ssq_ref[0:tm, :] = part
        ssq_ref[tm:, :] = jnp.zeros_like(ssq_ref[tm:, :])

    @pl.when(pl.program_id(1) > 0)
    def _():
        ssq_ref[0:tm, :] += part


def _proj_out(a, b, resid, tm):
    m, k = a.shape
    n = b.shape[1]
    tn = _tile(n, 512, _LANES)
    nt = m // tm
    rows = tm + 2 * _HALO
    vmem = (2 * _nbytes((tm, k), _BF16) + 2 * _nbytes((k, tn), _BF16) + 8 * _nbytes((tm, tn), _F32) + (6 << 20))
    return pl.pallas_call(
        _proj_out_body,
        out_shape=[jax.ShapeDtypeStruct((m, n), _F32), jax.ShapeDtypeStruct((nt, rows, n), _BF16),
                   jax.ShapeDtypeStruct((nt, rows, 1), _F32)],
        grid=(nt, n // tn),
        in_specs=[pl.BlockSpec((tm, k), lambda i, j: (i, 0)),
                  pl.BlockSpec((k, tn), lambda i, j: (0, j)),
                  pl.BlockSpec((tm, tn), lambda i, j: (i, j))],
        out_specs=[pl.BlockSpec((tm, tn), lambda i, j: (i, j)),
                   pl.BlockSpec((None, rows, tn), lambda i, j: (i, 0, j)),
                   pl.BlockSpec((None, rows, 1), lambda i, j: (i, 0, 0))],
        compiler_params=_params(("parallel", "arbitrary"), vmem),
        name="proj_out",
    )(a, b, resid)


def _conv_halo_body(xt_hbm, ssq_hbm, xt_out, ssq_out, x_buf, s_buf, sems, *, tm, tiles_per_seq):
    i = pl.program_id(0)
    nt = pl.num_programs(0)
    nxt = jnp.minimum(i + 1, nt - 1)
    prv = jnp.maximum(i - 1, 0)
    copies = [
        pltpu.make_async_copy(xt_hbm.at[nxt, pl.ds(0, _HALO)], x_buf.at[0], sems.at[0]),
        pltpu.make_async_copy(xt_hbm.at[prv, pl.ds(tm - _HALO, _HALO)], x_buf.at[1], sems.at[1]),
        pltpu.make_async_copy(ssq_hbm.at[nxt, pl.ds(0, _HALO)], s_buf.at[0], sems.at[2]),
        pltpu.make_async_copy(ssq_hbm.at[prv, pl.ds(tm - _HALO, _HALO)], s_buf.at[1], sems.at[3]),
    ]
    for copy in copies:
        copy.start()
    for copy in copies:
        copy.wait()
    pos = i % tiles_per_seq
    has_next = pos < tiles_per_seq - 1
    has_prev = pos > 0
    xt_out[0:_HALO, :] = jnp.where(has_next, x_buf[0], jnp.zeros_like(x_buf[0]))
    xt_out[_HALO:2 * _HALO, :] = jnp.where(has_prev, x_buf[1], jnp.zeros_like(x_buf[1]))
    ssq_out[0:_HALO, :] = jnp.where(has_next, s_buf[0], 0.0)
    ssq_out[_HALO:2 * _HALO, :] = jnp.where(has_prev, s_buf[1], 0.0)


def _conv_halo(x_tiles, ssq_tiles, seq, tm):
    nt, rows, d = x_tiles.shape
    tail = tm // (2 * _HALO)
    return pl.pallas_call(
        functools.partial(_conv_halo_body, tm=tm, tiles_per_seq=seq // tm),
        out_shape=[jax.ShapeDtypeStruct(x_tiles.shape, x_tiles.dtype),
                   jax.ShapeDtypeStruct(ssq_tiles.shape, ssq_tiles.dtype)],
        grid=(nt,),
        in_specs=[pl.BlockSpec(memory_space=pl.ANY), pl.BlockSpec(memory_space=pl.ANY)],
        out_specs=[pl.BlockSpec((None, 2 * _HALO, d), lambda i: (i, tail, 0)),
                   pl.BlockSpec((None, 2 * _HALO, 1), lambda i: (i, tail, 0))],
        scratch_shapes=[pltpu.VMEM((2, _HALO, d), x_tiles.dtype), pltpu.VMEM((2, _HALO, 1), ssq_tiles.dtype),
                        pltpu.SemaphoreType.DMA((4,))],
        input_output_aliases={0: 0, 1: 1},
        compiler_params=_params(("arbitrary",), 16 << 20),
        name="conv_halo",
    )(x_tiles, ssq_tiles)


def _proj_cast_body(a_ref, b_ref, o_ref):
    o_ref[...] = _dot(a_ref[...], b_ref[...]).astype(o_ref.dtype)


def _proj_sigmoid_body(a_ref, b_ref, o_ref):
    o_ref[...] = (0.5 * jnp.tanh(0.5 * _dot(a_ref[...], b_ref[...])) + 0.5).astype(o_ref.dtype)


def _proj_groupnorm_body(a_ref, b_ref, g_ref, o_ref, *, group):
    acc = _dot(a_ref[...], b_ref[...])
    for c in range(acc.shape[1] // group):
        cols = slice(c * group, (c + 1) * group)
        o_ref[:, cols] = _rms(acc[:, cols], g_ref[:, cols]).astype(o_ref.dtype)


def _proj_residual_body(a_ref, b_ref, r_ref, o_ref):
    o_ref[...] = r_ref[...] + _dot(a_ref[...], b_ref[...])


def _project(body, a, b, out_dtype, sections=None, extra=(), extra_specs=(), tile=1024, name="project"):
    m, k = a.shape
    sections = sections or ((0, b.shape[1]),)
    n = sum(width for _, width in sections)
    tm = _tile(m, tile, 16)
    tn = _tile(math.gcd(*(v for sec in sections for v in sec)), tile, _LANES)
    first_out, shifts, done = [], [], 0
    for start, width in sections:
        first_out.append(done // tn)
        shifts.append(start // tn - done // tn)
        done += width

    def src_block(j):
        shift = shifts[0]
        for first, s in zip(first_out[1:], shifts[1:]):
            shift = jnp.where(j >= first, s, shift)
        return j + shift

    vmem = (2 * _nbytes((tm, k), a.dtype) + 2 * _nbytes((k, tn), b.dtype)
            + 6 * _nbytes((tm, tn), _F32) + (4 << 20))
    return pl.pallas_call(
        body,
        out_shape=jax.ShapeDtypeStruct((m, n), out_dtype),
        grid=(m // tm, n // tn),
        in_specs=[pl.BlockSpec((tm, k), lambda i, j: (i, 0)),
                  pl.BlockSpec((k, tn), lambda i, j: (0, src_block(j))),
                  *[spec(tm, tn) for spec in extra_specs]],
        out_specs=pl.BlockSpec((tm, tn), lambda i, j: (i, j)),
        compiler_params=_params(("parallel", "arbitrary"), vmem),
        name=name,
    )(a, b, *extra)


def _row_vec_spec(tm, tn):
    return pl.BlockSpec((1, tn), lambda i, j: (0, j))


def _tile_spec(tm, tn):
    return pl.BlockSpec((tm, tn), lambda i, j: (i, j))


def _t5_bucket(rel, num_buckets):
    half = num_buckets // 2
    max_exact = half // 2
    ret = jnp.where(rel > 0, half, 0)
    n = jnp.abs(rel)
    nf = jnp.maximum(n, 1).astype(_F32)
    large = max_exact + (jnp.log(nf / max_exact) / math.log(_REL_MAX_DIST / max_exact)
                         * (half - max_exact)).astype(jnp.int32)
    large = jnp.minimum(large, half - 1)
    return ret + jnp.where(n < max_exact, n, large)


def _band_body(rb_ref, o_ref, *, tq, tk, m_lo, num_buckets):
    h = pl.program_id(0)
    mi = pl.program_id(1)
    last = pl.num_programs(1) - 1

    @pl.when(mi == 0)
    def _():
        o_ref[...] = jnp.full((tq, tk), rb_ref[h, num_buckets // 2 - 1] * _LOG2E, _F32)

    @pl.when(mi == last)
    def _():
        o_ref[...] = jnp.full((tq, tk), rb_ref[h, num_buckets - 1] * _LOG2E, _F32)

    @pl.when((mi > 0) & (mi < last))
    def _():
        row = lax.broadcasted_iota(jnp.int32, (tq, tk), 0)
        col = lax.broadcasted_iota(jnp.int32, (tq, tk), 1)
        bucket = _t5_bucket((mi + m_lo) * tq + col - row, num_buckets)
        out = jnp.zeros((tq, tk), _F32)
        for b in range(num_buckets):
            out = jnp.where(bucket == b, rb_ref[h, b], out)
        o_ref[...] = out * _LOG2E


def _band_range(tq, tk):
    m_lo = -((_REL_MAX_DIST - 1 + tk + tq - 1) // tq)
    m_hi = (_REL_MAX_DIST - 1 + tq + tq - 1) // tq
    return m_lo, m_hi


def _bias_band(rel_bias, tq, tk):
    nbk, h = rel_bias.shape
    m_lo, m_hi = _band_range(tq, tk)
    nb = m_hi - m_lo + 1
    return pl.pallas_call(
        functools.partial(_band_body, tq=tq, tk=tk, m_lo=m_lo, num_buckets=nbk),
        out_shape=jax.ShapeDtypeStruct((h, nb, tq, tk), _F32),
        grid=(h, nb),
        in_specs=[pl.BlockSpec(memory_space=pltpu.SMEM)],
        out_specs=pl.BlockSpec((None, None, tq, tk), lambda i, j: (i, j, 0, 0)),
        compiler_params=_params(("parallel", "parallel"), 24 * _nbytes((tq, tk), _F32) + (4 << 20)),
        name="bias_band",
    )(rel_bias.T)


def _lane_fold(x, op):
    return functools.reduce(op, [x[:, c:c + _LANES] for c in range(0, x.shape[1], _LANES)])


def _attn_body(lam_ref, q_ref, k_ref, v_ref, band_ref, g_ref, o_ref, s_ref, p_ref,
               *, tq, tk, d, m_lo, m_hi, lambda_init):
    i = pl.program_id(2)
    nk = k_ref.shape[0] // tk
    ratio = tk // tq

    band_idx = [jnp.clip(j * ratio - i, m_lo, m_hi) - m_lo for j in range(nk)]

    def scores(mp):
        cols = slice(mp * d, (mp + 1) * d)
        qm = q_ref[:, cols]
        for j in range(nk):
            keys = slice(j * tk, (j + 1) * tk)
            s_ref[mp, :, keys] = lax.dot_general(qm, k_ref[keys, cols], _NT, preferred_element_type=_F32)

    def softmax_rows(mp):
        sums = []
        for r in range(0, tq, _ROW_GROUP):
            rows = slice(r, r + _ROW_GROUP)
            part_max = None
            for j in range(nk):
                keys = slice(j * tk, (j + 1) * tk)
                folded = _lane_fold(s_ref[mp, rows, keys] + band_ref[band_idx[j], rows, :], jnp.maximum)
                part_max = folded if part_max is None else jnp.maximum(part_max, folded)
            row_max = jnp.max(part_max, axis=-1, keepdims=True)
            part_sum = jnp.zeros((_ROW_GROUP, _LANES), _F32)
            for j in range(nk):
                keys = slice(j * tk, (j + 1) * tk)
                p = jnp.exp2(s_ref[mp, rows, keys] + (band_ref[band_idx[j], rows, :] - row_max))
                part_sum = part_sum + _lane_fold(p, jnp.add)
                p_ref[mp, rows, keys] = p.astype(p_ref.dtype)
            sums.append(jnp.sum(part_sum, axis=-1, keepdims=True))
        return jnp.concatenate(sums, axis=0)

    scores(0)
    scores(1)
    heads_out = []
    for mp in range(2):
        row_sum = softmax_rows(mp)
        heads_out.append(_dot(p_ref[mp], v_ref[...]) / row_sum)

    lv = lam_ref[...]
    lam = (jnp.exp(jnp.sum(lv[0:1] * lv[1:2], axis=-1, keepdims=True))
           - jnp.exp(jnp.sum(lv[2:3] * lv[3:4], axis=-1, keepdims=True)) + lambda_init)
    o = heads_out[0] - lam * heads_out[1]
    o_ref[...] = (_rms(o, g_ref[...]) * (1.0 - lambda_init)).astype(o_ref.dtype)


def _diff_attention(qk, v_src, band, lam_vecs, subln_g, batch, seq, heads, d, lambda_init, tq, tk):
    m = qk.shape[0]
    nq = seq // tq
    m_lo, m_hi = _band_range(tq, tk)
    nb = m_hi - m_lo + 1
    w = 2 * d
    vmem = (4 * _nbytes((seq, w), _BF16) + 2 * _nbytes((nb, tq, tk), _F32) + _nbytes((2, tq, seq), _F32)
            + _nbytes((2, tq, seq), _BF16) + 8 * _nbytes((tq, tk), _F32) + (6 << 20))
    return pl.pallas_call(
        functools.partial(_attn_body, tq=tq, tk=tk, d=d, m_lo=m_lo, m_hi=m_hi, lambda_init=lambda_init),
        out_shape=jax.ShapeDtypeStruct((m, heads * w), _BF16),
        grid=(batch, heads, nq),
        in_specs=[
            pl.BlockSpec((4, d), lambda b, h, i: (0, 0)),
            pl.BlockSpec((tq, w), lambda b, h, i: (b * nq + i, h)),
            pl.BlockSpec((seq, w), lambda b, h, i: (b, heads + h)),
            pl.BlockSpec((seq, w), lambda b, h, i: (b, h)),
            pl.BlockSpec((None, nb, tq, tk), lambda b, h, i: (h, 0, 0, 0)),
            pl.BlockSpec((1, w), lambda b, h, i: (0, 0)),
        ],
        out_specs=pl.BlockSpec((tq, w), lambda b, h, i: (b * nq + i, h)),
        scratch_shapes=[pltpu.VMEM((2, tq, seq), _F32), pltpu.VMEM((2, tq, seq), _BF16)],
        compiler_params=_params(("parallel", "parallel", "arbitrary"), vmem),
        name="diff_attention",
    )(lam_vecs, qk, qk, v_src, band, subln_g.reshape(1, w))


def _split(x):
    hi = x.astype(_BF16)
    return hi, (x - hi.astype(_F32)).astype(_BF16)


def _split_dot(lhs, parts, dims=None):
    if dims is None:
        return sum(_dot(lhs, part) for part in parts)
    return sum(lax.dot_general(part, lhs, dims, preferred_element_type=_F32) for part in parts)


def _gla_body(*refs, tt, dk, dv, hpb, reverse, final, q_scale):
    if final:
        (q_ref, k_ref, lr_ref, v_ref, wg_ref, bg_ref, fwd_ref, r_ref, gn_ref, o_ref, s_ref) = refs
    else:
        (q_ref, k_ref, lr_ref, v_ref, wg_ref, bg_ref, o_ref, s_ref) = refs
    c_len = _GLA_CHUNK
    nc = tt // c_len

    @pl.when(pl.program_id(2) == 0)
    def _():
        s_ref[...] = jnp.zeros_like(s_ref)

    row = lax.broadcasted_iota(jnp.int32, (tt, tt), 0)
    col = lax.broadcasted_iota(jnp.int32, (tt, tt), 1)
    causal = ((row // c_len) == (col // c_len)) & ((col >= row) if reverse else (col <= row))
    causal_ones = causal.astype(_BF16)
    ones_cols = jnp.ones((tt, _LANES), _BF16)
    stems = lr_ref[...].astype(_BF16)
    order = list(reversed(range(nc))) if reverse else list(range(nc))
    edge = 0 if reverse else c_len - 1

    def chunk_rows(x, c):
        return x[c * c_len:(c + 1) * c_len]

    def per_chunk(fn):
        return jnp.concatenate([fn(c) for c in range(nc)], axis=0)

    for hh in range(hpb):
        kc = slice(hh * dk, (hh + 1) * dk)
        vc = slice(hh * dv, (hh + 1) * dv)
        v = v_ref[:, vc]
        pre = _dot(stems, wg_ref[:, kc]) + bg_ref[:, kc]
        log_a = (jnp.minimum(pre, 0.0) - jnp.log(1.0 + jnp.exp(-jnp.abs(pre)))) * (_LOG2E / _GLA_TAU)
        log_a_parts = _split(log_a)
        b = _split_dot(causal_ones, log_a_parts)
        total = [chunk_rows(b, c)[edge:edge + 1] for c in range(nc)]
        start, run = {}, jnp.zeros((1, dk), _F32)
        for c in order:
            start[c] = run
            run = run + total[c]

        q_own = q_ref[:, kc] * q_scale * jnp.exp2(b)
        k_own = k_ref[:, kc] * jnp.exp2(per_chunk(lambda c: total[c] - chunk_rows(b, c)))
        q_dec = q_own.astype(_BF16)
        k_inv = (k_ref[:, kc] * jnp.exp2(-b)).astype(_BF16)
        att = jnp.where(causal, lax.dot_general(q_dec, k_inv, _NT, preferred_element_type=_F32), 0.0)

        def seen_keys(c):
            pos = order.index(c)
            return per_chunk(lambda c2: (chunk_rows(k_own, c2) * jnp.exp2(start[c] - start[c2] - total[c2])
                                         if order.index(c2) < pos else jnp.zeros((c_len, dk), _F32)))

        att = att + per_chunk(lambda c: (
            jnp.zeros((c_len, tt), _F32) if c == order[0] else
            lax.dot_general(chunk_rows(q_dec, c), seen_keys(c).astype(_BF16), _NT, preferred_element_type=_F32)))

        s = s_ref[hh]
        q_state = per_chunk(lambda c: chunk_rows(q_own, c) * jnp.exp2(start[c])).astype(_BF16)
        k_state = per_chunk(lambda c: chunk_rows(k_own, c) * jnp.exp2(run - start[c] - total[c])).astype(_BF16)
        o = _dot(att.astype(_BF16), v) + _dot(q_state, s.astype(_BF16))
        tile_decay = jnp.exp2(_split_dot(ones_cols, log_a_parts, _TN))[:, 0:1]
        s_ref[hh] = tile_decay * s + lax.dot_general(k_state, v, _TN, preferred_element_type=_F32)
        if final:
            o = _rms(o + fwd_ref[:, vc], gn_ref[...])
            r = r_ref[:, vc].astype(_F32)
            o_ref[:, vc] = (o * (r * (0.5 * jnp.tanh(0.5 * r) + 0.5))).astype(o_ref.dtype)
        else:
            o_ref[:, vc] = o


def _gla(qk, stems, vgr, w_gate, b_gate, batch, seq, heads, dk, dv, v_col0, reverse,
         fwd=None, r_col0=None, norm_g=None):
    m = qk.shape[0]
    tt = _tile(seq, 256, _GLA_CHUNK)
    nt = seq // tt
    hpb = _tile(heads, 4, 1)
    ng = heads // hpb
    final = fwd is not None
    assert v_col0 % hpb == 0 and (r_col0 is None or r_col0 % hpb == 0)

    def rows(b, t):
        return b * nt + (nt - 1 - t if reverse else t)

    in_specs = [
        pl.BlockSpec((tt, hpb * dk), lambda b, g, t: (rows(b, t), g)),
        pl.BlockSpec((tt, hpb * dk), lambda b, g, t: (rows(b, t), ng + g)),
        pl.BlockSpec((tt, dk), lambda b, g, t: (rows(b, t), 0)),
        pl.BlockSpec((tt, hpb * dv), lambda b, g, t: (rows(b, t), v_col0 // hpb + g)),
        pl.BlockSpec((dk, hpb * dk), lambda b, g, t: (0, g)),
        pl.BlockSpec((1, hpb * dk), lambda b, g, t: (0, g)),
    ]
    args = [qk, qk, stems, vgr, w_gate, b_gate]
    if final:
        in_specs += [
            pl.BlockSpec((tt, hpb * dv), lambda b, g, t: (rows(b, t), g)),
            pl.BlockSpec((tt, hpb * dv), lambda b, g, t: (rows(b, t), r_col0 // hpb + g)),
            pl.BlockSpec((1, dv), lambda b, g, t: (0, 0)),
        ]
        args += [fwd, vgr, norm_g.reshape(1, dv)]
    return pl.pallas_call(
        functools.partial(_gla_body, tt=tt, dk=dk, dv=dv, hpb=hpb, reverse=reverse, final=final,
                          q_scale=dk ** -0.5),
        out_shape=jax.ShapeDtypeStruct((m, heads * dv), _BF16 if final else _F32),
        grid=(batch, ng, nt),
        in_specs=in_specs,
        out_specs=pl.BlockSpec((tt, hpb * dv), lambda b, g, t: (rows(b, t), g)),
        scratch_shapes=[pltpu.VMEM((hpb, dk, dv), _F32)],
        compiler_params=_params(("parallel", "parallel", "arbitrary"), 48 << 20),
        name="gla_bwd" if reverse else "gla_fwd",
    )(*args)


def _merge_body(oa_ref, ob_ref, pa_ref, pb_ref, ga_ref, gb_ref, o_ref):
    a = _dot(oa_ref[...], pa_ref[...])
    b = _dot(ob_ref[...], pb_ref[...])
    o_ref[...] = (ga_ref[...].astype(_F32) * a + gb_ref[...].astype(_F32) * b).astype(o_ref.dtype)


def _merge(o_a, o_b, p_a, p_b, gates):
    m, ka = o_a.shape
    kb = o_b.shape[1]
    n = p_a.shape[1]
    tm = _tile(m, 1024, 16)
    tn = _tile(n, 1024, _LANES)
    nj = n // tn
    vmem = (2 * _nbytes((tm, ka + kb), _BF16) + 2 * _nbytes((ka + kb, tn), _BF16)
            + 4 * _nbytes((tm, tn), _BF16) + 6 * _nbytes((tm, tn), _F32) + (4 << 20))
    return pl.pallas_call(
        _merge_body,
        out_shape=jax.ShapeDtypeStruct((m, n), _BF16),
        grid=(m // tm, nj),
        in_specs=[
            pl.BlockSpec((tm, ka), lambda i, j: (i, 0)),
            pl.BlockSpec((tm, kb), lambda i, j: (i, 0)),
            pl.BlockSpec((ka, tn), lambda i, j: (0, j)),
            pl.BlockSpec((kb, tn), lambda i, j: (0, j)),
            pl.BlockSpec((tm, tn), lambda i, j: (i, j)),
            pl.BlockSpec((tm, tn), lambda i, j: (i, nj + j)),
        ],
        out_specs=pl.BlockSpec((tm, tn), lambda i, j: (i, j)),
        compiler_params=_params(("parallel", "arbitrary"), vmem),
        name="branch_merge",
    )(o_a, o_b, p_a, p_b, gates, gates)


def _gelu_tanh(x):
    return x * (0.5 * (1.0 + jnp.tanh(math.sqrt(2.0 / math.pi) * (x + 0.044715 * (x * x * x)))))


def _ffn_up_weight_body(w_ref, gain_ref, o_ref, *, nblk):
    t = pl.program_id(0)
    q = 2 * (t // 4) + t % 2
    o_ref[...] = jnp.where(q < nblk, w_ref[...] * gain_ref[...], 0.0).astype(o_ref.dtype)


def _ffn_up_weight(w_up, f, row_gain):
    d = w_up.shape[0]
    blk = _FFN_TILE // 2
    assert f % blk == 0
    nblk = f // blk
    n_out = 4 * (-(-f // _FFN_TILE))

    def src(t):
        q = 2 * (t // 4) + t % 2
        return 0, jnp.minimum(jnp.where(t % 4 < 2, q, nblk + q), 2 * nblk - 1)

    return pl.pallas_call(
        functools.partial(_ffn_up_weight_body, nblk=nblk),
        out_shape=jax.ShapeDtypeStruct((d, n_out * blk), _BF16),
        grid=(n_out,),
        in_specs=[pl.BlockSpec((d, blk), src), pl.BlockSpec((d, 1), lambda t: (0, 0))],
        out_specs=pl.BlockSpec((d, blk), lambda t: (0, t)),
        compiler_params=_params(("parallel",), 6 * _nbytes((d, blk), _F32) + _nbytes((d, _LANES), _F32) + (4 << 20)),
        name="ffn_up_weight",
    )(w_up, row_gain.reshape(d, 1))


def _ffn_up_body(h_ref, ssq_ref, wa_ref, wb_ref, cw_ref, cb_ref, o_ref, *, tm, last_width):
    inv_rms = lax.rsqrt(ssq_ref[...] * (1.0 / h_ref.shape[1]) + _EPS)

    def gated(width):
        cols = slice(0, width)
        a = _dot(h_ref[...], wa_ref[:, cols]) * inv_rms
        g = _dot(h_ref[0:tm, :], wb_ref[:, cols]) * inv_rms[0:tm]
        rows = a.shape[0]
        prev = pltpu.roll(a, 1, 0)[0:tm]
        nxt = pltpu.roll(a, rows - 1, 0)[0:tm]
        conv = prev * cw_ref[0:1, cols] + a[0:tm] * cw_ref[1:2, cols] + nxt * cw_ref[2:3, cols] + cb_ref[:, cols]
        o_ref[:, cols] = (_gelu_tanh(conv) * g).astype(o_ref.dtype)

    tf = o_ref.shape[1]
    if last_width == tf:
        gated(tf)
    else:
        is_last = pl.program_id(1) == pl.num_programs(1) - 1
        pl.when(jnp.logical_not(is_last))(functools.partial(gated, tf))
        pl.when(is_last)(functools.partial(gated, last_width))


def _ffn_up(h_tiles, ssq_tiles, w_ag, conv_w, conv_b, tm, f):
    nt, rows, d = h_tiles.shape
    tf = _FFN_TILE
    vmem = (2 * _nbytes((rows, d), _BF16) + 4 * _nbytes((d, tf), _BF16) + 2 * _nbytes((tm, tf), _BF16)
            + 8 * _nbytes((rows, tf), _F32) + (4 << 20))
    return pl.pallas_call(
        functools.partial(_ffn_up_body, tm=tm, last_width=f % tf or tf),
        out_shape=jax.ShapeDtypeStruct((nt * tm, f), _BF16),
        grid=(nt, w_ag.shape[1] // (2 * tf)),
        in_specs=[
            pl.BlockSpec((None, rows, d), lambda i, j: (i, 0, 0)),
            pl.BlockSpec((None, rows, 1), lambda i, j: (i, 0, 0)),
            pl.BlockSpec((d, tf), lambda i, j: (0, 2 * j)),
            pl.BlockSpec((d, tf), lambda i, j: (0, 2 * j + 1)),
            pl.BlockSpec((3, tf), lambda i, j: (0, j)),
            pl.BlockSpec((1, tf), lambda i, j: (0, j)),
        ],
        out_specs=pl.BlockSpec((tm, tf), lambda i, j: (i, j)),
        compiler_params=_params(("parallel", "arbitrary"), vmem),
        name="ffn_up",
    )(h_tiles, ssq_tiles, w_ag, w_ag, conv_w, conv_b)


def _pad_cols(w, n):
    return jnp.pad(w, ((0, 0), (0, n - w.shape[1])))


def _prepare_weights(lambda_init, rel_bias, g_mix, w_in, q_norm_g, k_norm_g, lambda_q1, lambda_k1, lambda_q2,
                     lambda_k2, da_subln_g, w_gate_fwd, b_gate_fwd, w_gate_bwd, b_gate_bwd, gla_norm_g,
                     w_branch_a, w_branch_b, w_out, g_ffn, w_up, conv_w, conv_b, w_down):
    heads = rel_bias.shape[1]
    d = q_norm_g.shape[-1]
    da = heads * 2 * d
    rank, gla_k = w_gate_fwd.shape
    dv = gla_norm_g.shape[-1]
    gla_v = w_branch_b.shape[0]
    gh = gla_v // dv
    dk = gla_k // gh
    f = conv_b.shape[-1]
    d_model = w_in.shape[0]
    widths = [da, da, da, gla_k, gla_k, gla_v, gla_v, rank, rank, d_model, d_model]
    starts = [0]
    for wd in widths:
        starts.append(starts[-1] + wd)
    sec = lambda s: (starts[s], widths[s])

    w_stems = _pad_cols(w_in[:, starts[7]:starts[9]].astype(_BF16), dk)
    w_gates = w_in[:, starts[9]:].astype(_BF16)
    qk_gain = jnp.concatenate([jnp.tile(q_norm_g * (d ** -0.5 * _LOG2E), 2 * heads), jnp.tile(k_norm_g, 2 * heads)])

    def gate_weight(w, first_row):
        return jnp.zeros((dk, gla_k), _BF16).at[first_row:first_row + rank].set(w.astype(_BF16))

    fp = -(-f // _FFN_TILE) * _FFN_TILE
    return dict(
        heads=heads, d=d, gh=gh, dk=dk, dv=dv, lambda_init=lambda_init,
        rel_bias=rel_bias, g_mix=g_mix, g_ffn=g_ffn,
        w_in=w_in.astype(_BF16), sec_qk=(sec(0), sec(1)), sec_vgr=(sec(2), sec(5), sec(6)),
        sec_gla_qk=(sec(3), sec(4)),
        w_stems=w_stems, w_gates=w_gates, qk_gain=qk_gain.reshape(1, -1),
        lam_vecs=jnp.stack([lambda_q1, lambda_k1, lambda_q2, lambda_k2]),
        da_subln_g=da_subln_g, gla_norm_g=gla_norm_g,
        wg_fwd=gate_weight(w_gate_fwd, 0), wg_bwd=gate_weight(w_gate_bwd, rank),
        bg_fwd=b_gate_fwd.reshape(1, -1), bg_bwd=b_gate_bwd.reshape(1, -1),
        p_a=w_branch_a.astype(_BF16), p_b=w_branch_b.astype(_BF16), w_out=w_out.astype(_BF16),
        f=f, w_up=_ffn_up_weight(w_up, f, g_ffn),
        conv_w=_pad_cols(conv_w, fp), conv_b=_pad_cols(conv_b.reshape(1, f), fp),
        w_down=w_down.astype(_BF16),
    )


def _encoder_layer(x3, p, band, tq, tk):
    batch, seq, d_model = x3.shape
    x = x3.reshape(batch * seq, d_model)
    heads, d, gh, dk, dv = p["heads"], p["d"], p["gh"], p["dk"], p["dv"]

    h = _rmsnorm(x, p["g_mix"])
    qk = _project(functools.partial(_proj_groupnorm_body, group=d), h, p["w_in"], _BF16, p["sec_qk"],
                  extra=(p["qk_gain"],), extra_specs=(_row_vec_spec,), name="proj_qk")
    vgr = _project(_proj_cast_body, h, p["w_in"], _BF16, p["sec_vgr"], name="proj_vgr")
    gla_qk = _project(_proj_cast_body, h, p["w_in"], _F32, p["sec_gla_qk"], name="proj_gla_qk")
    stems = _project(_proj_cast_body, h, p["w_stems"], _F32, name="proj_stems")
    gates = _project(_proj_sigmoid_body, h, p["w_gates"], _BF16, name="proj_gates")

    o_a = _diff_attention(qk, vgr, band, p["lam_vecs"], p["da_subln_g"], batch, seq, heads, d,
                          p["lambda_init"], tq, tk)

    v_col0 = (heads * 2 * d) // dv
    r_col0 = v_col0 + gh
    fwd = _gla(gla_qk, stems, vgr, p["wg_fwd"], p["bg_fwd"], batch, seq, gh, dk, dv, v_col0, reverse=False)
    o_b = _gla(gla_qk, stems, vgr, p["wg_bwd"], p["bg_bwd"], batch, seq, gh, dk, dv, v_col0, reverse=True,
               fwd=fwd, r_col0=r_col0, norm_g=p["gla_norm_g"])

    merged = _merge(o_a, o_b, p["p_a"], p["p_b"], gates)
    tm = _tile(seq, 1024, 2 * _HALO)
    x1, x1_tiles, ssq_tiles = _proj_out(merged, p["w_out"], x, tm)
    x1_tiles, ssq_tiles = _conv_halo(x1_tiles, ssq_tiles, seq, tm)
    act = _ffn_up(x1_tiles, ssq_tiles, p["w_up"], p["conv_w"], p["conv_b"], tm, p["f"])
    y = _project(_proj_residual_body, act, p["w_down"], _F32, extra=(x1,), extra_specs=(_tile_spec,),
                 tile=_DOWN_TILE, name="ffn_down")
    return y.reshape(batch, seq, d_model)


def kernel(x_prompt, x_sample, rel_bias, g_mix, w_in, q_norm_g, k_norm_g, lambda_q1, lambda_k1, lambda_q2,
           lambda_k2, da_subln_g, w_gate_fwd, b_gate_fwd, w_gate_bwd, b_gate_bwd, gla_norm_g, w_branch_a,
           w_branch_b, w_out, g_ffn, w_up, conv_w, conv_b, w_down):
    layer_weights = (g_mix, w_in, q_norm_g, k_norm_g, lambda_q1, lambda_k1, lambda_q2, lambda_k2, da_subln_g,
                     w_gate_fwd, b_gate_fwd, w_gate_bwd, b_gate_bwd, gla_norm_g, w_branch_a, w_branch_b, w_out,
                     g_ffn, w_up, conv_w, conv_b, w_down)
    min_seq = min(x_prompt.shape[1], x_sample.shape[1])
    tq = _tile(min_seq, 512, _LANES)
    tk = _tile(min_seq, 512, tq)
    band = _bias_band(rel_bias, tq, tk)
    y_prompt, y_sample = x_prompt, x_sample
    for l in range(g_mix.shape[0]):
        lambda_init = 0.8 - 0.6 * math.exp(-0.3 * l)
        p = _prepare_weights(lambda_init, rel_bias, *(w[l] for w in layer_weights))
        y_prompt = _encoder_layer(y_prompt, p, band, tq, tk)
        y_sample = _encoder_layer(y_sample, p, band, tq, tk)
    return (y_prompt, y_sample)
```
